```python
import numpy as np
import jax
import jax.numpy as jnp
from jax import lax


D_MODEL = 2048
BATCH = 2
SEQ = 4096
DEPTH = 4
DEC_BATCH = 1
DEC_SEQ = 8192
PAST_LEN = 128

GRID_W = 64
NA_HEADS = 32
NA_HEAD_DIM = D_MODEL // NA_HEADS
NA_KH = 8
NA_KW = 16
NA_QCB = 16
NA_KB = NA_KW + NA_QCB
SSM_D_INNER = 2 * D_MODEL
SSM_HEAD_DIM = 64
SSM_HEADS = SSM_D_INNER // SSM_HEAD_DIM
SSM_GROUPS = 8
SSM_STATE = 128
SSM_CONV = 5
SSM_CHUNK = 128
SSM_GN = SSM_GROUPS * SSM_STATE
SSM_CONV_DIM = SSM_D_INNER + 2 * SSM_GN
SSM_IN_DIM = SSM_D_INNER + SSM_CONV_DIM + 2 * SSM_HEADS
MLP_HIDDEN = 4 * D_MODEL
N_MIXERS = 2
N_NA_LAYERS = (DEPTH + 1) // 2
N_SSM_LAYERS = DEPTH // 2
RMS_EPS = 1e-5

kernel_name = 'hybrid_natten_ssd_encoder'


def rms_norm(x, w):
    x32 = x.astype(jnp.float32)
    y = x32 * lax.rsqrt(jnp.mean(x32 * x32, axis=-1, keepdims=True) + RMS_EPS)
    return (y * w.astype(jnp.float32)).astype(x.dtype)


def neighbourhood_attention(h, qkv_w, qkv_b, rpb, out_w, out_b):
    b, t, _ = h.shape
    rows = t // GRID_W
    kh = min(NA_KH, rows)
    qkv = (h @ qkv_w + qkv_b).reshape(b, rows, GRID_W, 3, NA_HEADS, NA_HEAD_DIM)
    qkv = jnp.transpose(qkv, (3, 0, 4, 1, 2, 5))
    q = qkv[0] * (NA_HEAD_DIM ** -0.5)
    k, v = qkv[1], qkv[2]
    ncb = GRID_W // NA_QCB
    blk = np.arange(ncb)
    kb0 = np.clip(blk * NA_QCB - NA_KW // 2, 0, GRID_W - NA_KB)
    kcols = kb0[:, None] + np.arange(NA_KB)[None, :]
    qcols = blk[:, None] * NA_QCB + np.arange(NA_QCB)[None, :]
    cs = np.clip(qcols - NA_KW // 2, 0, GRID_W - NA_KW)
    kc = kcols[:, None, :]
    valid = (kc >= cs[..., None]) & (kc < cs[..., None] + NA_KW)
    dc = np.clip(kc - qcols[..., None], -(NA_KW - 1), NA_KW - 1) + (NA_KW - 1)
    col_bias = rpb[:, :, dc].astype(jnp.float32)
    valid_m = jnp.asarray(valid[:, :, None, :])

    def one_row(r):
        rs = jnp.clip(r - kh // 2, 0, rows - kh)
        qr = lax.dynamic_index_in_dim(q, r, axis=2, keepdims=False)
        qr = qr.reshape(b, NA_HEADS, ncb, NA_QCB, NA_HEAD_DIM)
        kr = lax.dynamic_slice_in_dim(k, rs, kh, axis=2)[:, :, :, kcols]
        vr = lax.dynamic_slice_in_dim(v, rs, kh, axis=2)[:, :, :, kcols]
        s = jnp.einsum('bhnqd,bhrnkd->bhnqrk', qr, kr).astype(jnp.float32)
        dr = rs + jnp.arange(kh) - r + (NA_KH - 1)
        bias = jnp.transpose(jnp.take(col_bias, dr, axis=1), (0, 2, 3, 1, 4))
        s = jnp.where(valid_m, s + bias[None], -1e30)
        p = jax.nn.softmax(s.reshape(s.shape[:4] + (kh * NA_KB,)), axis=-1).reshape(s.shape)
        o = jnp.einsum('bhnqrk,bhrnkd->bhnqd', p.astype(vr.dtype), vr)
        return o.reshape(b, NA_HEADS, GRID_W, NA_HEAD_DIM)

    o = lax.map(one_row, jnp.arange(rows))
    o = jnp.transpose(o, (1, 0, 3, 2, 4)).reshape(b, t, D_MODEL)
    return o @ out_w + out_b


def depthwise_conv(u, w, bias):
    c = u.shape[-1]
    out = lax.conv_general_dilated(
        u, w.astype(u.dtype)[:, None, :], window_strides=(1,),
        padding=[(SSM_CONV // 2, SSM_CONV // 2)],
        dimension_numbers=('NWC', 'WIO', 'NWC'), feature_group_count=c)
    return out + bias.astype(u.dtype)


def ssd_scan(x, dt, a, bm, cm):
    b, l, nh, p = x.shape
    g, n = bm.shape[2], bm.shape[3]
    j = nh // g
    c = l // SSM_CHUNK
    L = SSM_CHUNK
    xc = (x.astype(jnp.float32) * dt[..., None]).reshape(b, c, L, g, j, p)
    ad = jnp.transpose((dt * a).reshape(b, c, L, g, j), (0, 1, 3, 4, 2))
    a_cs = jnp.cumsum(ad, axis=-1)
    bc = bm.astype(jnp.float32).reshape(b, c, L, g, n)
    cc = cm.astype(jnp.float32).reshape(b, c, L, g, n)
    idx = jnp.arange(L)
    lower = idx[:, None] >= idx[None, :]
    seg = a_cs[..., :, None] - a_cs[..., None, :]
    decay_mat = jnp.exp(jnp.where(lower, seg, -jnp.inf))
    cb = jnp.einsum('bclgn,bcsgn->bcgls', cc, bc)
    y_diag = jnp.einsum('bcgjls,bcsgjp->bclgjp', cb[:, :, :, None] * decay_mat, xc)
    decay_states = jnp.transpose(jnp.exp(a_cs[..., -1:] - a_cs), (0, 1, 4, 2, 3))
    states = jnp.einsum('bcsgn,bcsgjp->bcgjpn', bc, xc * decay_states[..., None])
    chunk_decay = jnp.exp(a_cs[..., -1])

    def step(carry, inp):
        s, d = inp
        return carry * d[..., None, None] + s, carry

    init = jnp.zeros((b, g, j, p, n), jnp.float32)
    _, prev = lax.scan(step, init, (jnp.moveaxis(states, 1, 0), jnp.moveaxis(chunk_decay, 1, 0)))
    prev = jnp.moveaxis(prev, 0, 1)
    in_decay = jnp.transpose(jnp.exp(a_cs), (0, 1, 4, 2, 3))
    y_off = jnp.einsum('bclgn,bcgjpn->bclgjp', cc, prev) * in_decay[..., None]
    return (y_diag + y_off).reshape(b, l, nh, p)


def ssd_mixer(h, in_w, conv_w, conv_b, dt_bias, a_log, d_skip, norm_w, out_w):
    b, t, _ = h.shape
    zxbcdt = h @ in_w
    z = zxbcdt[..., :SSM_D_INNER]
    xbc = zxbcdt[..., SSM_D_INNER:SSM_D_INNER + SSM_CONV_DIM]
    dt = zxbcdt[..., SSM_D_INNER + SSM_CONV_DIM:]
    xbc = jax.nn.silu(depthwise_conv(xbc, conv_w, conv_b))
    xs = xbc[..., :SSM_D_INNER].reshape(b, t, SSM_HEADS, SSM_HEAD_DIM)
    bm = xbc[..., SSM_D_INNER:SSM_D_INNER + SSM_GN].reshape(b, t, SSM_GROUPS, SSM_STATE)
    cm = xbc[..., SSM_D_INNER + SSM_GN:].reshape(b, t, SSM_GROUPS, SSM_STATE)
    dt = jax.nn.softplus(dt.astype(jnp.float32).reshape(b, t, 2, SSM_HEADS)
                         + dt_bias.astype(jnp.float32))
    a = -jnp.exp(a_log.astype(jnp.float32))
    y_f = ssd_scan(xs, dt[:, :, 0], a[0], bm, cm)
    y_b = jnp.flip(ssd_scan(jnp.flip(xs, 1), jnp.flip(dt[:, :, 1], 1), a[1],
                            jnp.flip(bm, 1), jnp.flip(cm, 1)), 1)
    y = y_f + y_b + xs.astype(jnp.float32) * d_skip.astype(jnp.float32)[:, None]
    y = y.reshape(b, t, SSM_D_INNER) * jax.nn.silu(z.astype(jnp.float32))
    yg = y.reshape(b, t, SSM_GROUPS, SSM_D_INNER // SSM_GROUPS)
    yg = yg * lax.rsqrt(jnp.mean(yg * yg, axis=-1, keepdims=True) + RMS_EPS)
    y = yg.reshape(b, t, SSM_D_INNER) * norm_w.astype(jnp.float32)
    return y.astype(h.dtype) @ out_w


def trunk(x, mix_norm, na_qkv_w, na_qkv_b, na_rpb, na_out_w, na_out_b,
          ssm_in_w, ssm_conv_w, ssm_conv_b, ssm_dt_bias, ssm_a_log, ssm_d,
          ssm_norm_w, ssm_out_w, mlp_norm, mlp_up_w, mlp_down_w, final_norm):
    for i in range(DEPTH):
        j = i // N_MIXERS
        hn = rms_norm(x, mix_norm[i])
        if i % N_MIXERS == 0:
            x = x + neighbourhood_attention(hn, na_qkv_w[j], na_qkv_b[j], na_rpb[j],
                                            na_out_w[j], na_out_b[j])
        else:
            x = x + ssd_mixer(hn, ssm_in_w[j], ssm_conv_w[j], ssm_conv_b[j], ssm_dt_bias[j],
                              ssm_a_log[j], ssm_d[j], ssm_norm_w[j], ssm_out_w[j])
        hn = rms_norm(x, mlp_norm[i])
        x = x + jnp.square(jax.nn.relu(hn @ mlp_up_w[i])) @ mlp_down_w[i]
    return rms_norm(x, final_norm)


def setup_inputs(seed: int = 0) -> dict:
    key = jax.random.key(seed)
    ks = jax.random.split(key, 24)
    f32 = jnp.float32
    nrm = lambda k, shape, scale: jax.random.normal(k, shape, f32) * scale
    dt0 = jnp.exp(jax.random.uniform(ks[12], (N_SSM_LAYERS, 2, SSM_HEADS), f32,
                                     np.log(1e-3), np.log(1e-1)))
    return {
        'x_prompt': nrm(ks[0], (BATCH, SEQ, D_MODEL), 1.0),
        'x_sample': nrm(ks[1], (DEC_BATCH, DEC_SEQ, D_MODEL), 1.0),
        'mix_norm': 1.0 + nrm(ks[2], (DEPTH, D_MODEL), 0.01),
        'na_qkv_w': nrm(ks[3], (N_NA_LAYERS, D_MODEL, 3 * D_MODEL), D_MODEL ** -0.5),
        'na_qkv_b': nrm(ks[4], (N_NA_LAYERS, 3 * D_MODEL), 0.01),
        'na_rpb': nrm(ks[5], (N_NA_LAYERS, NA_HEADS, 2 * NA_KH - 1, 2 * NA_KW - 1), 0.1),
        'na_out_w': nrm(ks[6], (N_NA_LAYERS, D_MODEL, D_MODEL), D_MODEL ** -0.5),
        'na_out_b': nrm(ks[7], (N_NA_LAYERS, D_MODEL), 0.01),
        'ssm_in_w': nrm(ks[8], (N_SSM_LAYERS, D_MODEL, SSM_IN_DIM), D_MODEL ** -0.5),
        'ssm_conv_w': nrm(ks[9], (N_SSM_LAYERS, SSM_CONV, SSM_CONV_DIM), SSM_CONV ** -0.5),
        'ssm_conv_b': nrm(ks[10], (N_SSM_LAYERS, SSM_CONV_DIM), 0.01),
        'ssm_dt_bias': dt0 + jnp.log(-jnp.expm1(-dt0)),
        'ssm_a_log': jnp.log(jax.random.uniform(ks[13], (N_SSM_LAYERS, 2, SSM_HEADS), f32, 1.0, 16.0)),
        'ssm_d': 1.0 + nrm(ks[14], (N_SSM_LAYERS, SSM_HEADS), 0.01),
        'ssm_norm_w': 1.0 + nrm(ks[15], (N_SSM_LAYERS, SSM_D_INNER), 0.01),
        'ssm_out_w': nrm(ks[16], (N_SSM_LAYERS, SSM_D_INNER, D_MODEL), SSM_D_INNER ** -0.5),
        'mlp_norm': 1.0 + nrm(ks[17], (DEPTH, D_MODEL), 0.01),
        'mlp_up_w': nrm(ks[18], (DEPTH, D_MODEL, MLP_HIDDEN), D_MODEL ** -0.5),
        'mlp_down_w': nrm(ks[19], (DEPTH, MLP_HIDDEN, D_MODEL), MLP_HIDDEN ** -0.5),
        'final_norm': 1.0 + nrm(ks[20], (D_MODEL,), 0.01),
    }


def reference(x_prompt, x_sample, mix_norm, na_qkv_w, na_qkv_b, na_rpb, na_out_w, na_out_b,
              ssm_in_w, ssm_conv_w, ssm_conv_b, ssm_dt_bias, ssm_a_log, ssm_d,
              ssm_norm_w, ssm_out_w, mlp_norm, mlp_up_w, mlp_down_w, final_norm):
    y_prompt = trunk(x_prompt, mix_norm, na_qkv_w, na_qkv_b, na_rpb, na_out_w, na_out_b,
                     ssm_in_w, ssm_conv_w, ssm_conv_b, ssm_dt_bias, ssm_a_log, ssm_d,
                     ssm_norm_w, ssm_out_w, mlp_norm, mlp_up_w, mlp_down_w, final_norm)
    y_sample = trunk(x_sample, mix_norm, na_qkv_w, na_qkv_b, na_rpb, na_out_w, na_out_b,
                     ssm_in_w, ssm_conv_w, ssm_conv_b, ssm_dt_bias, ssm_a_log, ssm_d,
                     ssm_norm_w, ssm_out_w, mlp_norm, mlp_up_w, mlp_down_w, final_norm)
    return (y_prompt, y_sample)
```

```python
import functools

import numpy as np
import jax
import jax.numpy as jnp
from jax import lax
from jax.experimental import pallas as pl
from jax.experimental.pallas import tpu as pltpu

F32 = jnp.float32
BF16 = jnp.bfloat16

GRID_W = 64
NA_HEADS = 32
NA_HEAD_DIM = 64
NA_KH = 8
NA_KW = 16
SSM_HEAD_DIM = 64
SSM_GROUPS = 8
SSM_STATE = 128
SSM_CONV = 5
SSM_CHUNK = 128
N_MIXERS = 2
RMS_EPS = 1e-5

LANES = 128
SUBLANES = 8
VMEM_BYTES = 64 * 1024 * 1024
VMEM_CAP = VMEM_BYTES - 8 * 1024 * 1024


def _vmem_limit(pipelined_bytes, scratch_bytes=0, temp_bytes=0):
    need = 2 * pipelined_bytes + scratch_bytes + temp_bytes + (4 << 20)
    return int(min(max(need, 16 << 20), VMEM_CAP))


def _nbytes(shape, dtype):
    return int(np.prod(shape)) * jnp.dtype(dtype).itemsize


def _rms_scale(x):
    return x * lax.rsqrt(jnp.mean(x * x, axis=-1, keepdims=True) + RMS_EPS)


def _norm_matmul_kernel(x_ref, nw_ref, w_ref, b_ref, o_ref, hn_ref):
    @pl.when(pl.program_id(1) == 0)
    def _():
        hn_ref[...] = (_rms_scale(x_ref[...]) * nw_ref[...]).astype(BF16)

    acc = jnp.dot(hn_ref[...], w_ref[...], preferred_element_type=F32)
    o_ref[...] = (acc + b_ref[...]).astype(o_ref.dtype)


def norm_matmul(x, nw, w, b, out_dtype, *, tm, tn, name):
    t, d = x.shape
    n = w.shape[1]
    assert t % tm == 0 and n % tn == 0
    blocks = (_nbytes((tm, d), F32) + _nbytes((d, tn), BF16) + _nbytes((tm, tn), out_dtype))
    return pl.pallas_call(
        _norm_matmul_kernel,
        grid=(t // tm, n // tn),
        in_specs=[
            pl.BlockSpec((tm, d), lambda i, j: (i, 0)),
            pl.BlockSpec((1, d), lambda i, j: (0, 0)),
            pl.BlockSpec((d, tn), lambda i, j: (0, j)),
            pl.BlockSpec((1, tn), lambda i, j: (0, j)),
        ],
        out_specs=pl.BlockSpec((tm, tn), lambda i, j: (i, j)),
        out_shape=jax.ShapeDtypeStruct((t, n), out_dtype),
        scratch_shapes=[pltpu.VMEM((tm, d), BF16)],
        compiler_params=pltpu.CompilerParams(
            dimension_semantics=("parallel", "arbitrary"),
            vmem_limit_bytes=_vmem_limit(blocks, _nbytes((tm, d), BF16), _nbytes((tm, tn), F32))),
        name=name,
    )(x, nw.reshape(1, d), w, b.reshape(1, n))


def _matmul_res_kernel(a_ref, w_ref, b_ref, r_ref, o_ref):
    acc = jnp.dot(a_ref[...], w_ref[...], preferred_element_type=F32)
    o_ref[...] = r_ref[...] + (acc + b_ref[...])


def matmul_res(a, w, b, res, *, tm, tn, name):
    t, k = a.shape
    n = w.shape[1]
    assert t % tm == 0 and n % tn == 0
    blocks = (_nbytes((tm, k), BF16) + _nbytes((k, tn), BF16) + 2 * _nbytes((tm, tn), F32))
    return pl.pallas_call(
        _matmul_res_kernel,
        grid=(t // tm, n // tn),
        in_specs=[
            pl.BlockSpec((tm, k), lambda i, j: (i, 0)),
            pl.BlockSpec((k, tn), lambda i, j: (0, j)),
            pl.BlockSpec((1, tn), lambda i, j: (0, j)),
            pl.BlockSpec((tm, tn), lambda i, j: (i, j)),
        ],
        out_specs=pl.BlockSpec((tm, tn), lambda i, j: (i, j)),
        out_shape=jax.ShapeDtypeStruct((t, n), F32),
        compiler_params=pltpu.CompilerParams(
            dimension_semantics=("parallel", "arbitrary"),
            vmem_limit_bytes=_vmem_limit(blocks, 0, _nbytes((tm, tn), F32))),
        name=name,
    )(a, w, b.reshape(1, n), res)


def _mlp_kernel(x_ref, nw_ref, wu_ref, wd_ref, fw_ref, o_ref, hn_ref, *, final):
    k = pl.program_id(1)

    @pl.when(k == 0)
    def _():
        x = x_ref[...]
        hn_ref[...] = (_rms_scale(x) * nw_ref[...]).astype(BF16)
        o_ref[...] = x

    h = jnp.dot(hn_ref[...], wu_ref[...], preferred_element_type=F32)
    h = jnp.square(jnp.maximum(h, 0.0)).astype(BF16)
    o_ref[...] += jnp.dot(h, wd_ref[...], preferred_element_type=F32)

    if final:
        @pl.when(k == pl.num_programs(1) - 1)
        def _():
            o_ref[...] = _rms_scale(o_ref[...]) * fw_ref[...]


def mlp(x, nw, wu, wd, fw, *, final, tm, th, name):
    t, d = x.shape
    hid = wu.shape[1]
    assert t % tm == 0 and hid % th == 0
    blocks = (2 * _nbytes((tm, d), F32) + _nbytes((d, th), BF16) + _nbytes((th, d), BF16))
    return pl.pallas_call(
        functools.partial(_mlp_kernel, final=final),
        grid=(t // tm, hid // th),
        in_specs=[
            pl.BlockSpec((tm, d), lambda i, k: (i, 0)),
            pl.BlockSpec((1, d), lambda i, k: (0, 0)),
            pl.BlockSpec((d, th), lambda i, k: (0, k)),
            pl.BlockSpec((th, d), lambda i, k: (k, 0)),
            pl.BlockSpec((1, d), lambda i, k: (0, 0)),
        ],
        out_specs=pl.BlockSpec((tm, d), lambda i, k: (i, 0)),
        out_shape=jax.ShapeDtypeStruct((t, d), F32),
        scratch_shapes=[pltpu.VMEM((tm, d), BF16)],
        compiler_params=pltpu.CompilerParams(
            dimension_semantics=("parallel", "arbitrary"),
            vmem_limit_bytes=_vmem_limit(blocks, _nbytes((tm, d), BF16),
                                         _nbytes((tm, th), F32) + _nbytes((tm, d), F32))),
        name=name,
    )(x, nw.reshape(1, d), wu, wd, fw.reshape(1, d))


NA_WIN_TOK = NA_KH * GRID_W


def _na_kernel(q_ref, k_ref, v_ref, bias_ref, mask_ref, o_ref, *, rows):
    lane = lax.broadcasted_iota(jnp.int32, (GRID_W, LANES), 1)
    first_head = lane < NA_HEAD_DIM

    def body(r, carry):
        rs = jnp.clip(r - NA_KH // 2, 0, rows - NA_KH)
        delta = r - rs
        q0 = pl.multiple_of(r * GRID_W, GRID_W)
        k0 = pl.multiple_of(rs * GRID_W, GRID_W)
        q = q_ref[pl.ds(q0, GRID_W), :] * (NA_HEAD_DIM ** -0.5)
        zero = jnp.zeros_like(q)
        qs = jnp.concatenate([jnp.where(first_head, q, zero), jnp.where(first_head, zero, q)], axis=0)
        kk = k_ref[pl.ds(k0, NA_WIN_TOK), :]
        s = lax.dot_general(qs, kk, (((1,), (1,)), ((), ())), preferred_element_type=F32)
        s = jnp.where(mask_ref[...] > 0.0, s + bias_ref[delta], -1e30)
        m = jnp.max(s, axis=-1, keepdims=True)
        p = jnp.exp(s - m)
        l = jnp.sum(p, axis=-1, keepdims=True)
        vv = v_ref[pl.ds(k0, NA_WIN_TOK), :]
        pv = jnp.dot(p.astype(BF16), vv, preferred_element_type=F32) / l
        out = jnp.where(first_head, pv[:GRID_W], pv[GRID_W:])
        o_ref[pl.ds(q0, GRID_W), :] = out.astype(o_ref.dtype)
        return carry

    lax.fori_loop(0, rows, body, 0)


def na_core(qkv, bias_tab, mask, *, seq_len, seq_block0, n_seq, name):
    d = qkv.shape[1] // 3
    n_pairs = d // LANES
    rows = seq_len // GRID_W
    blk = (seq_len, LANES)
    blocks = 4 * _nbytes(blk, BF16) + _nbytes(bias_tab.shape[1:], F32) + _nbytes(mask.shape, F32)
    return pl.pallas_call(
        functools.partial(_na_kernel, rows=rows),
        grid=(n_pairs, n_seq),
        in_specs=[
            pl.BlockSpec(blk, lambda h, b: (seq_block0 + b, h)),
            pl.BlockSpec(blk, lambda h, b: (seq_block0 + b, n_pairs + h)),
            pl.BlockSpec(blk, lambda h, b: (seq_block0 + b, 2 * n_pairs + h)),
            pl.BlockSpec((None,) + bias_tab.shape[1:], lambda h, b: (h, 0, 0, 0)),
            pl.BlockSpec(mask.shape, lambda h, b: (0, 0)),
        ],
        out_specs=pl.BlockSpec(blk, lambda h, b: (b, h)),
        out_shape=jax.ShapeDtypeStruct((n_seq * seq_len, d), BF16),
        compiler_params=pltpu.CompilerParams(
            dimension_semantics=("parallel", "arbitrary"),
            vmem_limit_bytes=_vmem_limit(blocks, 0, 8 << 20)),
        name=name,
    )(qkv, qkv, qkv, bias_tab, mask)


def _na_bias_tables(rpb):
    h = rpb.shape[0]
    c = np.arange(GRID_W)[:, None]
    kc = np.arange(GRID_W)[None, :]
    dc = np.clip(kc - c, -(NA_KW - 1), NA_KW - 1) + (NA_KW - 1)
    dr = np.arange(NA_KH)[None, :] - np.arange(NA_KH)[:, None] + (NA_KH - 1)
    tab = rpb.astype(F32)[:, dr[:, None, :, None], dc[None, :, None, :]]
    tab = tab.reshape(h // 2, 2, NA_KH, GRID_W, NA_KH * GRID_W)
    return jnp.transpose(tab, (0, 2, 1, 3, 4)).reshape(h // 2, NA_KH, 2 * GRID_W, NA_KH * GRID_W)


def _na_mask():
    c = np.arange(GRID_W)[:, None]
    kc = np.arange(GRID_W)[None, :]
    cs = np.clip(c - NA_KW // 2, 0, GRID_W - NA_KW)
    valid = ((kc >= cs) & (kc < cs + NA_KW)).astype(np.float32)
    return jnp.asarray(np.tile(valid, (2, NA_KH)))


def _split3(x):
    hi = x.astype(BF16)
    r1 = x - hi.astype(F32)
    mid = r1.astype(BF16)
    lo = (r1 - mid.astype(F32)).astype(BF16)
    return hi, mid, lo


def _dt_kernel(raw_ref, bias_ref, alog_ref, dt_ref, a_ref, at_ref, *, n_chunks, n_heads):
    L = SSM_CHUNK
    x = raw_ref[...] + bias_ref[...]
    dt = jnp.maximum(x, 0.0) + jnp.log1p(jnp.exp(-jnp.abs(x)))
    dt_ref[...] = dt
    ad = dt * (-jnp.exp(alog_ref[...]))
    row = lax.broadcasted_iota(jnp.int32, (L, L), 0)
    col = lax.broadcasted_iota(jnp.int32, (L, L), 1)
    tri_f = (col <= row).astype(BF16)
    tri_b = (col >= row).astype(BF16)
    fwd_lane = lax.broadcasted_iota(jnp.int32, (L, 2 * n_heads), 1) < n_heads
    for c in range(n_chunks):
        parts = _split3(ad[c * L:(c + 1) * L])
        f = sum(jnp.dot(tri_f, p, preferred_element_type=F32) for p in parts)
        b = sum(jnp.dot(tri_b, p, preferred_element_type=F32) for p in parts)
        a = jnp.where(fwd_lane, f, b)
        a_ref[c * L:(c + 1) * L, :] = a
        at_ref[c] = a.T


def ssd_steps(raw, dt_bias, a_log, *, tb, name):
    t, w = raw.shape
    n_chunks = tb // SSM_CHUNK
    assert t % tb == 0 and w == LANES
    blocks = 3 * _nbytes((tb, w), F32) + _nbytes((n_chunks, w, SSM_CHUNK), F32)
    return pl.pallas_call(
        functools.partial(_dt_kernel, n_chunks=n_chunks, n_heads=w // 2),
        grid=(t // tb,),
        in_specs=[
            pl.BlockSpec((tb, w), lambda i: (i, 0)),
            pl.BlockSpec((1, w), lambda i: (0, 0)),
            pl.BlockSpec((1, w), lambda i: (0, 0)),
        ],
        out_specs=[
            pl.BlockSpec((tb, w), lambda i: (i, 0)),
            pl.BlockSpec((tb, w), lambda i: (i, 0)),
            pl.BlockSpec((n_chunks, w, SSM_CHUNK), lambda i: (i, 0, 0)),
        ],
        out_shape=[
            jax.ShapeDtypeStruct((t, w), F32),
            jax.ShapeDtypeStruct((t, w), F32),
            jax.ShapeDtypeStruct((t // SSM_CHUNK, w, SSM_CHUNK), F32),
        ],
        compiler_params=pltpu.CompilerParams(
            dimension_semantics=("parallel",),
            vmem_limit_bytes=_vmem_limit(blocks, 0, 4 << 20)),
        name=name,
    )(raw, dt_bias.reshape(1, w), a_log.reshape(1, w))


HALO = SUBLANES


def _any_equal(v, values):
    return functools.reduce(jnp.logical_or, [v == s for s in values])


def _conv_kernel(prev_ref, main_ref, next_ref, w_ref, b_ref, o_ref, buf_ref, *, tb, starts, ends):
    start = pl.program_id(0) * tb
    at_start = _any_equal(start, starts)
    at_end = _any_equal(start + tb, ends)
    buf_ref[0:HALO, :] = jnp.where(at_start, 0.0, prev_ref[...])
    buf_ref[HALO:HALO + tb, :] = main_ref[...]
    buf_ref[HALO + tb:2 * HALO + tb, :] = jnp.where(at_end, 0.0, next_ref[...])
    acc = jnp.zeros(o_ref.shape, F32) + b_ref[...]
    for k in range(SSM_CONV):
        off = HALO - SSM_CONV // 2 + k
        acc = acc + buf_ref[off:off + tb, :] * w_ref[k:k + 1, :]
    o_ref[...] = (acc / (1.0 + jnp.exp(-acc))).astype(o_ref.dtype)


def ssd_conv(zx, conv_w, conv_b, *, col0, tb, tc, starts, ends, name):
    t = zx.shape[0]
    c = conv_w.shape[1]
    assert t % tb == 0 and c % tc == 0 and col0 % tc == 0 and tb % HALO == 0
    cb0 = col0 // tc
    hb = tb // HALO
    n_hb = t // HALO
    blocks = (_nbytes((tb, tc), F32) + _nbytes((tb, tc), BF16))
    return pl.pallas_call(
        functools.partial(_conv_kernel, tb=tb, starts=starts, ends=ends),
        grid=(t // tb, c // tc),
        in_specs=[
            pl.BlockSpec((HALO, tc), lambda i, j: (jnp.maximum(i * hb - 1, 0), cb0 + j)),
            pl.BlockSpec((tb, tc), lambda i, j: (i, cb0 + j)),
            pl.BlockSpec((HALO, tc), lambda i, j: (jnp.minimum((i + 1) * hb, n_hb - 1), cb0 + j)),
            pl.BlockSpec((SSM_CONV, tc), lambda i, j: (0, j)),
            pl.BlockSpec((1, tc), lambda i, j: (0, j)),
        ],
        out_specs=pl.BlockSpec((tb, tc), lambda i, j: (i, j)),
        out_shape=jax.ShapeDtypeStruct((t, c), BF16),
        scratch_shapes=[pltpu.VMEM((tb + 2 * HALO, tc), F32)],
        compiler_params=pltpu.CompilerParams(
            dimension_semantics=("parallel", "arbitrary"),
            vmem_limit_bytes=_vmem_limit(blocks, _nbytes((tb + 2 * HALO, tc), F32), 4 * _nbytes((tb, tc), F32))),
        name=name,
    )(zx, zx, zx, conv_w, conv_b.reshape(1, c))


def _scan_kernel(x_ref, b_ref, c_ref, dt_ref, a_ref, at_ref, *rest, tb, bounds, reverse):
    if reverse:
        yf_ref, z_ref, dsk_ref, nw_ref, o_ref, state_ref = rest
    else:
        o_ref, state_ref = rest
    L = SSM_CHUNK
    P = SSM_HEAD_DIM
    n_chunks = tb // L
    heads = x_ref.shape[1] // P
    i = pl.program_id(1)
    blk = (pl.num_programs(1) - 1 - i) if reverse else i
    edge = (blk + 1) * tb if reverse else blk * tb

    @pl.when(_any_equal(edge, bounds))
    def _():
        state_ref[...] = jnp.zeros_like(state_ref)

    row = lax.broadcasted_iota(jnp.int32, (L, L), 0)
    col = lax.broadcasted_iota(jnp.int32, (L, L), 1)
    tri = (row <= col) if reverse else (row >= col)
    lane = lax.broadcasted_iota(jnp.int32, (L, LANES), 1)
    first_head = lane < P
    lane_head = lax.broadcasted_iota(jnp.int32, (1, heads * P), 1) // P

    chunks = range(n_chunks - 1, -1, -1) if reverse else range(n_chunks)
    for c in chunks:
        rows = slice(c * L, (c + 1) * L)
        x = x_ref[rows, :]
        bm = b_ref[rows, :]
        cm = c_ref[rows, :]
        dt = dt_ref[rows, :]
        a = a_ref[rows, :]
        at = at_ref[c]
        e_idx = 0 if reverse else L - 1
        a_end = a[e_idx:e_idx + 1, :]
        w = dt * jnp.exp(a_end - a)
        ea = jnp.exp(a)
        cb = lax.dot_general(cm, bm, (((1,), (1,)), ((), ())), preferred_element_type=F32)
        cmf = cm.astype(F32)
        state = state_ref[...]
        e_row = jnp.zeros((1, heads * P), F32)
        xw_parts = []
        for pair in range(heads // 2):
            cols = slice(pair * LANES, (pair + 1) * LANES)
            xp = x[:, cols].astype(F32)
            h0, h1 = 2 * pair, 2 * pair + 1
            dt_p = jnp.where(first_head, dt[:, h0:h0 + 1], dt[:, h1:h1 + 1])
            w_p = jnp.where(first_head, w[:, h0:h0 + 1], w[:, h1:h1 + 1])
            xw_parts.append((xp * w_p).astype(BF16))
            rhs = jnp.concatenate([(xp * dt_p).astype(BF16), state[:, cols].astype(BF16)], axis=0)
            ys = []
            for h in (h0, h1):
                seg = a[:, h:h + 1] - at[h:h + 1, :]
                m = cb * jnp.exp(jnp.where(tri, seg, -jnp.inf))
                lhs = jnp.concatenate([m.astype(BF16), (cmf * ea[:, h:h + 1]).astype(BF16)], axis=1)
                ys.append(jnp.dot(lhs, rhs, preferred_element_type=F32))
                e_row = jnp.where(lane_head == h, jnp.exp(a_end[:, h:h + 1]), e_row)
            o_ref[rows, cols] = jnp.where(first_head, ys[0], ys[1]).astype(o_ref.dtype)
        xw = jnp.concatenate(xw_parts, axis=1)
        z_new = lax.dot_general(bm, xw, (((0,), (0,)), ((), ())), preferred_element_type=F32)
        state_ref[...] = state * e_row + z_new

        if reverse:
            y = o_ref[rows, :] + yf_ref[rows, :] + x.astype(F32) * dsk_ref[...]
            zz = z_ref[rows, :]
            y = y * (zz / (1.0 + jnp.exp(-zz)))
            o_ref[rows, :] = _rms_scale(y) * nw_ref[...]


def ssd_scan(xc, dtg, ag, at, extra, *, d_inner, tb, bounds, reverse, name):
    t = xc.shape[0]
    G = SSM_GROUPS
    gw = d_inner // G
    hpg = gw // SSM_HEAD_DIM
    n_blk = t // tb
    n_chunks = tb // SSM_CHUNK
    assert t % tb == 0 and gw % LANES == 0
    xb0 = d_inner // SSM_STATE
    cb0 = xb0 + G
    at_row0 = (at.shape[1] // 2 // hpg) if reverse else 0

    def tok(i):
        return (n_blk - 1 - i) if reverse else i

    in_specs = [
        pl.BlockSpec((tb, gw), lambda g, i: (tok(i), g)),
        pl.BlockSpec((tb, SSM_STATE), lambda g, i: (tok(i), xb0 + g)),
        pl.BlockSpec((tb, SSM_STATE), lambda g, i: (tok(i), cb0 + g)),
        pl.BlockSpec((None, tb, hpg), lambda g, i: (g, tok(i), 0)),
        pl.BlockSpec((None, tb, hpg), lambda g, i: (g, tok(i), 0)),
        pl.BlockSpec((n_chunks, hpg, SSM_CHUNK), lambda g, i: (tok(i), at_row0 + g, 0)),
    ]
    args = [xc, xc, xc, dtg, ag, at]
    blocks = (_nbytes((tb, gw), BF16) + 2 * _nbytes((tb, SSM_STATE), BF16)
              + 2 * _nbytes((tb, LANES), F32) + _nbytes((n_chunks, hpg, SSM_CHUNK), F32)
              + _nbytes((tb, gw), F32))
    if reverse:
        yf, zx, dsk, nw = extra
        in_specs += [
            pl.BlockSpec((tb, gw), lambda g, i: (tok(i), g)),
            pl.BlockSpec((tb, gw), lambda g, i: (tok(i), g)),
            pl.BlockSpec((1, gw), lambda g, i: (0, g)),
            pl.BlockSpec((1, gw), lambda g, i: (0, g)),
        ]
        args += [yf, zx, dsk, nw]
        blocks += 2 * _nbytes((tb, gw), F32)
    return pl.pallas_call(
        functools.partial(_scan_kernel, tb=tb, bounds=bounds, reverse=reverse),
        grid=(G, n_blk),
        in_specs=in_specs,
        out_specs=pl.BlockSpec((tb, gw), lambda g, i: (tok(i), g)),
        out_shape=jax.ShapeDtypeStruct((t, d_inner), F32),
        scratch_shapes=[pltpu.VMEM((SSM_STATE, gw), F32)],
        compiler_params=pltpu.CompilerParams(
            dimension_semantics=("parallel", "arbitrary"),
            vmem_limit_bytes=_vmem_limit(blocks, _nbytes((SSM_STATE, gw), F32), 8 << 20)),
        name=name,
    )(*args)


def _mixer_na(x, nw, qkv_w, qkv_b, rpb, out_w, out_b, seqs, li):
    qkv = norm_matmul(x, nw, qkv_w.astype(BF16), qkv_b, BF16, tm=1024, tn=1024, name=f"na_qkv_{li}")
    bias_tab = _na_bias_tables(rpb)
    mask = _na_mask()
    outs = []
    tok0 = 0
    for si, (seq_len, n_seq) in enumerate(seqs):
        assert tok0 % seq_len == 0
        outs.append(na_core(qkv, bias_tab, mask, seq_len=seq_len, seq_block0=tok0 // seq_len,
                            n_seq=n_seq, name=f"na_core_{li}_{si}"))
        tok0 += seq_len * n_seq
    o = jnp.concatenate(outs, axis=0)
    return matmul_res(o, out_w.astype(BF16), out_b, x, tm=1024, tn=1024, name=f"na_out_{li}")


def _mixer_ssd(x, nw, in_w, conv_w, conv_b, dt_bias, a_log, d_skip, norm_w, out_w, starts, ends, li):
    d = x.shape[1]
    d_inner = out_w.shape[0]
    conv_dim = conv_w.shape[1]
    n_main = d_inner + conv_dim
    n_dt = in_w.shape[1] - n_main
    heads = d_inner // SSM_HEAD_DIM
    hpg = heads // SSM_GROUPS
    t = x.shape[0]
    zero_b = jnp.zeros((n_main,), F32)
    zx = norm_matmul(x, nw, in_w[:, :n_main].astype(BF16), zero_b, F32, tm=1024, tn=1024, name=f"ssd_in_{li}")
    raw = norm_matmul(x, nw, in_w[:, n_main:].astype(BF16), jnp.zeros((n_dt,), F32), F32,
                      tm=1024, tn=n_dt, name=f"ssd_dt_{li}")
    dt, a, at = ssd_steps(raw, dt_bias, a_log, tb=1024, name=f"ssd_steps_{li}")
    xc = ssd_conv(zx, conv_w, conv_b, col0=d_inner, tb=512, tc=512, starts=starts, ends=ends, name=f"ssd_conv_{li}")

    def per_group(v):
        return jnp.transpose(v.reshape(t, 2, SSM_GROUPS, hpg), (1, 2, 0, 3))

    dtg = per_group(dt)
    ag = per_group(a)
    yf = ssd_scan(xc, dtg[0], ag[0], at, None, d_inner=d_inner, tb=512, bounds=starts, reverse=False,
                  name=f"ssd_fwd_{li}")
    dsk = jnp.repeat(d_skip.astype(F32), SSM_HEAD_DIM).reshape(1, d_inner)
    y = ssd_scan(xc, dtg[1], ag[1], at, (yf, zx, dsk, norm_w.reshape(1, d_inner)), d_inner=d_inner,
                 tb=512, bounds=ends, reverse=True, name=f"ssd_bwd_{li}")
    return matmul_res(y.astype(BF16), out_w.astype(BF16), jnp.zeros((d,), F32), x, tm=1024, tn=1024,
                      name=f"ssd_out_{li}")


def kernel(x_prompt, x_sample, mix_norm, na_qkv_w, na_qkv_b, na_rpb, na_out_w, na_out_b, ssm_in_w, ssm_conv_w, ssm_conv_b, ssm_dt_bias, ssm_a_log, ssm_d, ssm_norm_w, ssm_out_w, mlp_norm, mlp_up_w, mlp_down_w, final_norm):
    bp, tp, d = x_prompt.shape
    bs, ts, _ = x_sample.shape
    depth = mix_norm.shape[0]
    x = jnp.concatenate([x_prompt.reshape(bp * tp, d), x_sample.reshape(bs * ts, d)], axis=0)
    seqs = ((tp, bp), (ts, bs))
    starts, ends = [], []
    tok = 0
    for seq_len, n_seq in seqs:
        for _ in range(n_seq):
            starts.append(tok)
            tok += seq_len
            ends.append(tok)
    starts, ends = tuple(starts), tuple(ends)

    for i in range(depth):
        j = i // N_MIXERS
        if i % N_MIXERS == 0:
            x = _mixer_na(x, mix_norm[i], na_qkv_w[j], na_qkv_b[j], na_rpb[j], na_out_w[j], na_out_b[j], seqs, i)
        else:
            x = _mixer_ssd(x, mix_norm[i], ssm_in_w[j], ssm_conv_w[j], ssm_conv_b[j], ssm_dt_bias[j],
                           ssm_a_log[j], ssm_d[j], ssm_norm_w[j], ssm_out_w[j], starts, ends, i)
        x = mlp(x, mlp_norm[i], mlp_up_w[i].astype(BF16), mlp_down_w[i].astype(BF16), final_norm,
                final=(i == depth - 1), tm=1024, th=512, name=f"mlp_{i}")
    y_prompt = x[:bp * tp].reshape(bp, tp, d)
    y_sample = x[bp * tp:].reshape(bs, ts, d)
    return (y_prompt, y_sample)
```

```python
import functools

import numpy as np
import jax
import jax.numpy as jnp
from jax import lax
from jax.experimental import pallas as pl
from jax.experimental.pallas import tpu as pltpu

F32 = jnp.float32
BF16 = jnp.bfloat16

GRID_W = 64
NA_HEADS = 32
NA_HEAD_DIM = 64
NA_KH = 8
NA_KW = 16
SSM_HEAD_DIM = 64
SSM_GROUPS = 8
SSM_STATE = 128
SSM_CONV = 5
SSM_CHUNK = 128
N_MIXERS = 2
RMS_EPS = 1e-5

LANES = 128
SUBLANES = 8
VMEM_BYTES = 64 * 1024 * 1024
VMEM_CAP = VMEM_BYTES - 8 * 1024 * 1024


def _vmem_limit(pipelined_bytes, scratch_bytes=0, temp_bytes=0):
    need = 2 * pipelined_bytes + scratch_bytes + temp_bytes + (4 << 20)
    return int(min(max(need, 16 << 20), VMEM_CAP))


def _nbytes(shape, dtype):
    return int(np.prod(shape)) * jnp.dtype(dtype).itemsize


def _rms_scale(x):
    return x * lax.rsqrt(jnp.mean(x * x, axis=-1, keepdims=True) + RMS_EPS)


def _norm_matmul_kernel(x_ref, nw_ref, w_ref, b_ref, o_ref, hn_ref):
    @pl.when(pl.program_id(1) == 0)
    def _():
        hn_ref[...] = (_rms_scale(x_ref[...]) * nw_ref[...]).astype(BF16)

    acc = jnp.dot(hn_ref[...], w_ref[...], preferred_element_type=F32)
    o_ref[...] = (acc + b_ref[...]).astype(o_ref.dtype)


def norm_matmul(x, nw, w, b, out_dtype, *, tm, tn, name):
    t, d = x.shape
    n = w.shape[1]
    assert t % tm == 0 and n % tn == 0
    blocks = (_nbytes((tm, d), F32) + _nbytes((d, tn), BF16) + _nbytes((tm, tn), out_dtype))
    return pl.pallas_call(
        _norm_matmul_kernel,
        grid=(t // tm, n // tn),
        in_specs=[
            pl.BlockSpec((tm, d), lambda i, j: (i, 0)),
            pl.BlockSpec((1, d), lambda i, j: (0, 0)),
            pl.BlockSpec((d, tn), lambda i, j: (0, j)),
            pl.BlockSpec((1, tn), lambda i, j: (0, j)),
        ],
        out_specs=pl.BlockSpec((tm, tn), lambda i, j: (i, j)),
        out_shape=jax.ShapeDtypeStruct((t, n), out_dtype),
        scratch_shapes=[pltpu.VMEM((tm, d), BF16)],
        compiler_params=pltpu.CompilerParams(
            dimension_semantics=("parallel", "arbitrary"),
            vmem_limit_bytes=_vmem_limit(blocks, _nbytes((tm, d), BF16), _nbytes((tm, tn), F32))),
        name=name,
    )(x, nw.reshape(1, d), w, b.reshape(1, n))


def _matmul_res_kernel(a_ref, w_ref, b_ref, r_ref, o_ref):
    acc = jnp.dot(a_ref[...], w_ref[...], preferred_element_type=F32)
    o_ref[...] = r_ref[...] + (acc + b_ref[...])


def matmul_res(a, w, b, res, *, tm, tn, name):
    t, k = a.shape
    n = w.shape[1]
    assert t % tm == 0 and n % tn == 0
    blocks = (_nbytes((tm, k), BF16) + _nbytes((k, tn), BF16) + 2 * _nbytes((tm, tn), F32))
    return pl.pallas_call(
        _matmul_res_kernel,
        grid=(t // tm, n // tn),
        in_specs=[
            pl.BlockSpec((tm, k), lambda i, j: (i, 0)),
            pl.BlockSpec((k, tn), lambda i, j: (0, j)),
            pl.BlockSpec((1, tn), lambda i, j: (0, j)),
            pl.BlockSpec((tm, tn), lambda i, j: (i, j)),
        ],
        out_specs=pl.BlockSpec((tm, tn), lambda i, j: (i, j)),
        out_shape=jax.ShapeDtypeStruct((t, n), F32),
        compiler_params=pltpu.CompilerParams(
            dimension_semantics=("parallel", "arbitrary"),
            vmem_limit_bytes=_vmem_limit(blocks, 0, _nbytes((tm, tn), F32))),
        name=name,
    )(a, w, b.reshape(1, n), res)


def _mlp_kernel(x_ref, nw_ref, wu_ref, wd_ref, fw_ref, o_ref, hn_ref, *, final):
    k = pl.program_id(1)

    @pl.when(k == 0)
    def _():
        x = x_ref[...]
        hn_ref[...] = (_rms_scale(x) * nw_ref[...]).astype(BF16)
        o_ref[...] = x

    h = jnp.dot(hn_ref[...], wu_ref[...], preferred_element_type=F32)
    h = jnp.square(jnp.maximum(h, 0.0)).astype(BF16)
    o_ref[...] += jnp.dot(h, wd_ref[...], preferred_element_type=F32)

    if final:
        @pl.when(k == pl.num_programs(1) - 1)
        def _():
            o_ref[...] = _rms_scale(o_ref[...]) * fw_ref[...]


def mlp(x, nw, wu, wd, fw, *, final, tm, th, name):
    t, d = x.shape
    hid = wu.shape[1]
    assert t % tm == 0 and hid % th == 0
    blocks = (2 * _nbytes((tm, d), F32) + _nbytes((d, th), BF16) + _nbytes((th, d), BF16))
    return pl.pallas_call(
        functools.partial(_mlp_kernel, final=final),
        grid=(t // tm, hid // th),
        in_specs=[
            pl.BlockSpec((tm, d), lambda i, k: (i, 0)),
            pl.BlockSpec((1, d), lambda i, k: (0, 0)),
            pl.BlockSpec((d, th), lambda i, k: (0, k)),
            pl.BlockSpec((th, d), lambda i, k: (k, 0)),
            pl.BlockSpec((1, d), lambda i, k: (0, 0)),
        ],
        out_specs=pl.BlockSpec((tm, d), lambda i, k: (i, 0)),
        out_shape=jax.ShapeDtypeStruct((t, d), F32),
        scratch_shapes=[pltpu.VMEM((tm, d), BF16)],
        compiler_params=pltpu.CompilerParams(
            dimension_semantics=("parallel", "arbitrary"),
            vmem_limit_bytes=_vmem_limit(blocks, _nbytes((tm, d), BF16),
                                         _nbytes((tm, th), F32) + _nbytes((tm, d), F32))),
        name=name,
    )(x, nw.reshape(1, d), wu, wd, fw.reshape(1, d))


NA_WIN_TOK = NA_KH * GRID_W
NA_N_DR = 2 * NA_KH - 1
NA_ROWS_PER_ITER = 4


def _na_kernel(q_ref, k_ref, v_ref, rp_ref, o_ref, u_ref, *, rows):
    W = GRID_W
    lane = lax.broadcasted_iota(jnp.int32, (W, LANES), 1)
    first_head = lane < NA_HEAD_DIM
    even_row = lane < W

    c = lax.broadcasted_iota(jnp.int32, (W, LANES), 0)
    kc = jnp.where(even_row, lane, lane - W)
    cs = jnp.clip(c - NA_KW // 2, 0, W - NA_KW)
    valid = (kc >= cs) & (kc < cs + NA_KW)
    shift = LANES - (NA_KW - 1)
    for h in range(2):
        t0_prev = None
        for dr in range(NA_N_DR):
            base = jnp.broadcast_to(rp_ref[h, dr:dr + 1, :], (W, LANES))
            t0 = pltpu.roll(base, shift, 1, stride=1, stride_axis=0)
            if dr >= 1:
                t1 = pltpu.roll(base, (shift + W) % LANES, 1, stride=1, stride_axis=0)
                u_ref[h, dr - 1] = jnp.where(valid, jnp.where(even_row, t0_prev, t1), -jnp.inf)
            t0_prev = t0

    def scores(r):
        rs = jnp.clip(r - NA_KH // 2, 0, rows - NA_KH)
        q0 = pl.multiple_of(r * W, W)
        k0 = pl.multiple_of(rs * W, W)
        q = q_ref[pl.ds(q0, W), :] * (NA_HEAD_DIM ** -0.5)
        zero = jnp.zeros_like(q)
        qs = jnp.concatenate([jnp.where(first_head, q, zero), jnp.where(first_head, zero, q)], axis=0)
        kk = k_ref[pl.ds(k0, NA_WIN_TOK), :]
        return lax.dot_general(qs, kk, (((1,), (1,)), ((), ())), preferred_element_type=F32)

    def attend(r, s):
        rs = jnp.clip(r - NA_KH // 2, 0, rows - NA_KH)
        d0 = (NA_KH - 1) - (r - rs)
        q0 = pl.multiple_of(r * W, W)
        k0 = pl.multiple_of(rs * W, W)
        bias = jnp.concatenate(
            [jnp.concatenate([u_ref[h, d0 + 2 * j] for j in range(NA_KH // 2)], axis=1) for h in range(2)],
            axis=0)
        s = s + bias
        m = jnp.max(s, axis=-1, keepdims=True)
        p = jnp.exp(s - m)
        l = jnp.sum(p, axis=-1, keepdims=True)
        vv = v_ref[pl.ds(k0, NA_WIN_TOK), :]
        pv = jnp.dot(p.astype(BF16), vv, preferred_element_type=F32) / l
        out = jnp.where(first_head, pv[:W], pv[W:])
        o_ref[pl.ds(q0, W), :] = out.astype(o_ref.dtype)

    def body(it, carry):
        rs_ = [it * NA_ROWS_PER_ITER + u for u in range(NA_ROWS_PER_ITER)]
        ss = [scores(r) for r in rs_]
        for r, s in zip(rs_, ss):
            attend(r, s)
        return carry

    lax.fori_loop(0, rows // NA_ROWS_PER_ITER, body, 0)


def na_core(qkv, rp, *, seq_len, seq_block0, n_seq, name):
    d = qkv.shape[1] // 3
    n_pairs = d // LANES
    rows = seq_len // GRID_W
    assert rows % NA_ROWS_PER_ITER == 0 and rows >= NA_KH
    blk = (seq_len, LANES)
    u_shape = (2, NA_N_DR - 1, GRID_W, LANES)
    blocks = 4 * _nbytes(blk, BF16) + _nbytes((2,) + rp.shape[1:], F32)
    return pl.pallas_call(
        functools.partial(_na_kernel, rows=rows),
        grid=(n_pairs, n_seq),
        in_specs=[
            pl.BlockSpec(blk, lambda h, b: (seq_block0 + b, h)),
            pl.BlockSpec(blk, lambda h, b: (seq_block0 + b, n_pairs + h)),
            pl.BlockSpec(blk, lambda h, b: (seq_block0 + b, 2 * n_pairs + h)),
            pl.BlockSpec((2,) + rp.shape[1:], lambda h, b: (h, 0, 0)),
        ],
        out_specs=pl.BlockSpec(blk, lambda h, b: (b, h)),
        out_shape=jax.ShapeDtypeStruct((n_seq * seq_len, d), BF16),
        scratch_shapes=[pltpu.VMEM(u_shape, F32)],
        compiler_params=pltpu.CompilerParams(
            dimension_semantics=("parallel", "arbitrary"),
            vmem_limit_bytes=_vmem_limit(blocks, _nbytes(u_shape, F32), 16 << 20)),
        name=name,
    )(qkv, qkv, qkv, rp)


def _na_pad_rpb(rpb):
    h, n_dr, n_dc = rpb.shape
    return jnp.pad(rpb.astype(F32), ((0, 0), (0, 2 * NA_KH - n_dr), (0, LANES - n_dc)))


def _split3(x):
    hi = x.astype(BF16)
    r1 = x - hi.astype(F32)
    mid = r1.astype(BF16)
    lo = (r1 - mid.astype(F32)).astype(BF16)
    return hi, mid, lo


def _dt_kernel(raw_ref, bias_ref, alog_ref, a_ref, ea_ref, at_ref, dtt_ref, wt_ref, *, n_chunks, n_heads):
    L = SSM_CHUNK
    x = raw_ref[...] + bias_ref[...]
    dt = jnp.maximum(x, 0.0) + jnp.log1p(jnp.exp(-jnp.abs(x)))
    ad = dt * (-jnp.exp(alog_ref[...]))
    row = lax.broadcasted_iota(jnp.int32, (L, L), 0)
    col = lax.broadcasted_iota(jnp.int32, (L, L), 1)
    tri_f = (col <= row).astype(BF16)
    tri_b = (col >= row).astype(BF16)
    fwd_lane = lax.broadcasted_iota(jnp.int32, (L, 2 * n_heads), 1) < n_heads
    for c in range(n_chunks):
        rows = slice(c * L, (c + 1) * L)
        parts = _split3(ad[rows])
        f = sum(jnp.dot(tri_f, p, preferred_element_type=F32) for p in parts)
        b = sum(jnp.dot(tri_b, p, preferred_element_type=F32) for p in parts)
        a = jnp.where(fwd_lane, f, b)
        a_end = jnp.where(fwd_lane[:1], f[L - 1:L], b[0:1])
        a_ref[rows, :] = a
        ea_ref[rows, :] = jnp.exp(a)
        at_ref[c] = a.T
        dtt_ref[c] = dt[rows].T
        wt_ref[c] = (dt[rows] * jnp.exp(a_end - a)).T


def ssd_steps(raw, dt_bias, a_log, *, tb, name):
    t, w = raw.shape
    n_chunks = tb // SSM_CHUNK
    assert t % tb == 0 and w == LANES
    tok_spec = pl.BlockSpec((tb, w), lambda i: (i, 0))
    hm_spec = pl.BlockSpec((n_chunks, w, SSM_CHUNK), lambda i: (i, 0, 0))
    hm_shape = jax.ShapeDtypeStruct((t // SSM_CHUNK, w, SSM_CHUNK), F32)
    blocks = 3 * _nbytes((tb, w), F32) + 3 * _nbytes((n_chunks, w, SSM_CHUNK), F32)
    return pl.pallas_call(
        functools.partial(_dt_kernel, n_chunks=n_chunks, n_heads=w // 2),
        grid=(t // tb,),
        in_specs=[
            tok_spec,
            pl.BlockSpec((1, w), lambda i: (0, 0)),
            pl.BlockSpec((1, w), lambda i: (0, 0)),
        ],
        out_specs=[tok_spec, tok_spec, hm_spec, hm_spec, hm_spec],
        out_shape=[jax.ShapeDtypeStruct((t, w), F32), jax.ShapeDtypeStruct((t, w), F32),
                   hm_shape, hm_shape, hm_shape],
        compiler_params=pltpu.CompilerParams(
            dimension_semantics=("parallel",),
            vmem_limit_bytes=_vmem_limit(blocks, 0, 4 << 20)),
        name=name,
    )(raw, dt_bias.reshape(1, w), a_log.reshape(1, w))


HALO = SUBLANES
SSD_SCAN_TB = 512


def _any_equal(v, values):
    return functools.reduce(jnp.logical_or, [v == s for s in values])


def _conv_kernel(prev_ref, main_ref, next_ref, w_ref, b_ref, o_ref, buf_ref, *, tb, starts, ends):
    start = pl.program_id(0) * tb
    at_start = _any_equal(start, starts)
    at_end = _any_equal(start + tb, ends)
    buf_ref[0:HALO, :] = jnp.where(at_start, 0.0, prev_ref[...])
    buf_ref[HALO:HALO + tb, :] = main_ref[...]
    buf_ref[HALO + tb:2 * HALO + tb, :] = jnp.where(at_end, 0.0, next_ref[...])
    acc = jnp.zeros(o_ref.shape, F32) + b_ref[...]
    for k in range(SSM_CONV):
        off = HALO - SSM_CONV // 2 + k
        acc = acc + buf_ref[off:off + tb, :] * w_ref[k:k + 1, :]
    o_ref[...] = (acc / (1.0 + jnp.exp(-acc))).astype(o_ref.dtype)


def ssd_conv(zx, conv_w, conv_b, *, col0, tb, tc, starts, ends, name):
    t = zx.shape[0]
    c = conv_w.shape[1]
    assert t % tb == 0 and c % tc == 0 and col0 % tc == 0 and tb % HALO == 0
    cb0 = col0 // tc
    hb = tb // HALO
    n_hb = t // HALO
    blocks = (_nbytes((tb, tc), F32) + _nbytes((tb, tc), BF16))
    return pl.pallas_call(
        functools.partial(_conv_kernel, tb=tb, starts=starts, ends=ends),
        grid=(t // tb, c // tc),
        in_specs=[
            pl.BlockSpec((HALO, tc), lambda i, j: (jnp.maximum(i * hb - 1, 0), cb0 + j)),
            pl.BlockSpec((tb, tc), lambda i, j: (i, cb0 + j)),
            pl.BlockSpec((HALO, tc), lambda i, j: (jnp.minimum((i + 1) * hb, n_hb - 1), cb0 + j)),
            pl.BlockSpec((SSM_CONV, tc), lambda i, j: (0, j)),
            pl.BlockSpec((1, tc), lambda i, j: (0, j)),
        ],
        out_specs=pl.BlockSpec((tb, tc), lambda i, j: (i, j)),
        out_shape=jax.ShapeDtypeStruct((t, c), BF16),
        scratch_shapes=[pltpu.VMEM((tb + 2 * HALO, tc), F32)],
        compiler_params=pltpu.CompilerParams(
            dimension_semantics=("parallel", "arbitrary"),
            vmem_limit_bytes=_vmem_limit(blocks, _nbytes((tb + 2 * HALO, tc), F32), 4 * _nbytes((tb, tc), F32))),
        name=name,
    )(zx, zx, zx, conv_w, conv_b.reshape(1, c))


def _scan_kernel(x_ref, b_ref, c_ref, a_ref, ea_ref, at_ref, dtt_ref, wt_ref, *rest, tb, bounds, reverse):
    if reverse:
        yf_ref, z_ref, dsk_ref, nw_ref, o_ref, state_ref, y_ref = rest
    else:
        o_ref, state_ref = rest
        y_ref = o_ref
    L = SSM_CHUNK
    P = SSM_HEAD_DIM
    N = SSM_STATE
    n_chunks = tb // L
    heads = x_ref.shape[1] // P
    i = pl.program_id(1)
    blk = (pl.num_programs(1) - 1 - i) if reverse else i
    edge = (blk + 1) * tb if reverse else blk * tb

    @pl.when(_any_equal(edge, bounds))
    def _():
        state_ref[...] = jnp.zeros_like(state_ref)

    row = lax.broadcasted_iota(jnp.int32, (L, L), 0)
    col = lax.broadcasted_iota(jnp.int32, (L, L), 1)
    tri = (row <= col) if reverse else (row >= col)
    first_head = lax.broadcasted_iota(jnp.int32, (L, LANES), 1) < P
    first_head_n = lax.broadcasted_iota(jnp.int32, (N, LANES), 1) < P
    e_idx = 0 if reverse else L - 1

    def chunk(j, carry):
        c = (n_chunks - 1 - j) if reverse else j
        rows = pl.ds(pl.multiple_of(c * L, L), L)
        x = x_ref[rows, :]
        bm = b_ref[rows, :]
        cm = c_ref[rows, :]
        a = a_ref[rows, :]
        ea = ea_ref[rows, :]
        at = at_ref[c]
        dtt = dtt_ref[c]
        wt = wt_ref[c]
        a_end = a[e_idx:e_idx + 1, :]
        cb = lax.dot_general(cm, bm, (((1,), (1,)), ((), ())), preferred_element_type=F32)
        cmf = cm.astype(F32)
        btf = bm.astype(F32).T
        for pair in range(heads // 2):
            cols = slice(pair * LANES, (pair + 1) * LANES)
            xp = x[:, cols]
            st = state_ref[:, cols]
            rhs = jnp.concatenate([xp, st.astype(BF16)], axis=0)
            ys, bts, es = [], [], []
            for h in (2 * pair, 2 * pair + 1):
                seg = a[:, h:h + 1] - at[h:h + 1, :]
                m = (cb * dtt[h:h + 1, :]) * jnp.exp(jnp.where(tri, seg, -jnp.inf))
                lhs = jnp.concatenate([m.astype(BF16), (cmf * ea[:, h:h + 1]).astype(BF16)], axis=1)
                ys.append(jnp.dot(lhs, rhs, preferred_element_type=F32))
                bts.append((btf * wt[h:h + 1, :]).astype(BF16))
                es.append(jnp.exp(a_end[:, h:h + 1]))
            zs = jnp.dot(jnp.concatenate(bts, axis=0), xp, preferred_element_type=F32)
            z_pair = jnp.where(first_head_n, zs[:N], zs[N:])
            e_pair = jnp.where(first_head_n[:1], es[0], es[1])
            state_ref[:, cols] = st * e_pair + z_pair
            y_pair = jnp.where(first_head, ys[0], ys[1])
            if reverse:
                y_ref[:, cols] = y_pair
            else:
                y_ref[rows, cols] = y_pair

        if reverse:
            y = y_ref[...] + yf_ref[rows, :] + x.astype(F32) * dsk_ref[...]
            zz = z_ref[rows, :]
            y = y * (zz / (1.0 + jnp.exp(-zz)))
            o_ref[rows, :] = (_rms_scale(y) * nw_ref[...]).astype(o_ref.dtype)
        return carry

    lax.fori_loop(0, n_chunks, chunk, 0)


def ssd_scan(xc, ag, eag, at, dtt, wt, extra, *, d_inner, tb, bounds, reverse, name):
    t = xc.shape[0]
    G = SSM_GROUPS
    gw = d_inner // G
    hpg = gw // SSM_HEAD_DIM
    n_blk = t // tb
    n_chunks = tb // SSM_CHUNK
    assert t % tb == 0 and gw % LANES == 0
    xb0 = d_inner // SSM_STATE
    cb0 = xb0 + G
    at_row0 = (at.shape[1] // 2 // hpg) if reverse else 0

    def tok(i):
        return (n_blk - 1 - i) if reverse else i

    tok_spec = pl.BlockSpec((None, tb, hpg), lambda g, i: (g, tok(i), 0))
    hm_spec = pl.BlockSpec((n_chunks, hpg, SSM_CHUNK), lambda g, i: (tok(i), at_row0 + g, 0))
    in_specs = [
        pl.BlockSpec((tb, gw), lambda g, i: (tok(i), g)),
        pl.BlockSpec((tb, SSM_STATE), lambda g, i: (tok(i), xb0 + g)),
        pl.BlockSpec((tb, SSM_STATE), lambda g, i: (tok(i), cb0 + g)),
        tok_spec, tok_spec, hm_spec, hm_spec, hm_spec,
    ]
    args = [xc, xc, xc, ag, eag, at, dtt, wt]
    blocks = (_nbytes((tb, gw), BF16) + 2 * _nbytes((tb, SSM_STATE), BF16)
              + 2 * _nbytes((tb, LANES), F32) + 3 * _nbytes((n_chunks, hpg, SSM_CHUNK), F32)
              + _nbytes((tb, gw), F32))
    if reverse:
        yf, zx, dsk, nw = extra
        in_specs += [
            pl.BlockSpec((tb, gw), lambda g, i: (tok(i), g)),
            pl.BlockSpec((tb, gw), lambda g, i: (tok(i), g)),
            pl.BlockSpec((1, gw), lambda g, i: (0, g)),
            pl.BlockSpec((1, gw), lambda g, i: (0, g)),
        ]
        args += [yf, zx, dsk, nw]
        blocks += 2 * _nbytes((tb, gw), F32)
    scratch = [pltpu.VMEM((SSM_STATE, gw), F32)]
    if reverse:
        scratch.append(pltpu.VMEM((SSM_CHUNK, gw), F32))
    return pl.pallas_call(
        functools.partial(_scan_kernel, tb=tb, bounds=bounds, reverse=reverse),
        grid=(G, n_blk),
        in_specs=in_specs,
        out_specs=pl.BlockSpec((tb, gw), lambda g, i: (tok(i), g)),
        out_shape=jax.ShapeDtypeStruct((t, d_inner), BF16 if reverse else F32),
        scratch_shapes=scratch,
        compiler_params=pltpu.CompilerParams(
            dimension_semantics=("parallel", "arbitrary"),
            vmem_limit_bytes=_vmem_limit(blocks, _nbytes((SSM_STATE, gw), F32), 8 << 20)),
        name=name,
    )(*args)


def _mixer_na(x, nw, qkv_w, qkv_b, rpb, out_w, out_b, seqs, li):
    qkv = norm_matmul(x, nw, qkv_w.astype(BF16), qkv_b, BF16, tm=1024, tn=1024, name=f"na_qkv_{li}")
    rp = _na_pad_rpb(rpb)
    outs = []
    tok0 = 0
    for si, (seq_len, n_seq) in enumerate(seqs):
        assert tok0 % seq_len == 0
        outs.append(na_core(qkv, rp, seq_len=seq_len, seq_block0=tok0 // seq_len,
                            n_seq=n_seq, name=f"na_core_{li}_{si}"))
        tok0 += seq_len * n_seq
    o = jnp.concatenate(outs, axis=0)
    return matmul_res(o, out_w.astype(BF16), out_b, x, tm=1024, tn=1024, name=f"na_out_{li}")


def _mixer_ssd(x, nw, in_w, conv_w, conv_b, dt_bias, a_log, d_skip, norm_w, out_w, starts, ends, li):
    d = x.shape[1]
    d_inner = out_w.shape[0]
    conv_dim = conv_w.shape[1]
    n_main = d_inner + conv_dim
    n_dt = in_w.shape[1] - n_main
    heads = d_inner // SSM_HEAD_DIM
    hpg = heads // SSM_GROUPS
    t = x.shape[0]
    zero_b = jnp.zeros((n_main,), F32)
    zx = norm_matmul(x, nw, in_w[:, :n_main].astype(BF16), zero_b, F32, tm=1024, tn=1024, name=f"ssd_in_{li}")
    raw = norm_matmul(x, nw, in_w[:, n_main:].astype(BF16), jnp.zeros((n_dt,), F32), F32,
                      tm=1024, tn=n_dt, name=f"ssd_dt_{li}")
    a, ea, at, dtt, wt = ssd_steps(raw, dt_bias, a_log, tb=1024, name=f"ssd_steps_{li}")
    xc = ssd_conv(zx, conv_w, conv_b, col0=d_inner, tb=512, tc=512, starts=starts, ends=ends, name=f"ssd_conv_{li}")

    def per_group(v):
        return jnp.transpose(v.reshape(t, 2, SSM_GROUPS, hpg), (1, 2, 0, 3))

    ag = per_group(a)
    eag = per_group(ea)
    yf = ssd_scan(xc, ag[0], eag[0], at, dtt, wt, None, d_inner=d_inner, tb=SSD_SCAN_TB, bounds=starts,
                  reverse=False, name=f"ssd_fwd_{li}")
    dsk = jnp.repeat(d_skip.astype(F32), SSM_HEAD_DIM).reshape(1, d_inner)
    y = ssd_scan(xc, ag[1], eag[1], at, dtt, wt, (yf, zx, dsk, norm_w.reshape(1, d_inner)), d_inner=d_inner,
                 tb=SSD_SCAN_TB, bounds=ends, reverse=True, name=f"ssd_bwd_{li}")
    return matmul_res(y, out_w.astype(BF16), jnp.zeros((d,), F32), x, tm=1024, tn=1024,
                      name=f"ssd_out_{li}")


def kernel(x_prompt, x_sample, mix_norm, na_qkv_w, na_qkv_b, na_rpb, na_out_w, na_out_b, ssm_in_w, ssm_conv_w, ssm_conv_b, ssm_dt_bias, ssm_a_log, ssm_d, ssm_norm_w, ssm_out_w, mlp_norm, mlp_up_w, mlp_down_w, final_norm):
    bp, tp, d = x_prompt.shape
    bs, ts, _ = x_sample.shape
    depth = mix_norm.shape[0]
    x = jnp.concatenate([x_prompt.reshape(bp * tp, d), x_sample.reshape(bs * ts, d)], axis=0)
    seqs = ((tp, bp), (ts, bs))
    starts, ends = [], []
    tok = 0
    for seq_len, n_seq in seqs:
        for _ in range(n_seq):
            starts.append(tok)
            tok += seq_len
            ends.append(tok)
    starts, ends = tuple(starts), tuple(ends)

    for i in range(depth):
        j = i // N_MIXERS
        if i % N_MIXERS == 0:
            x = _mixer_na(x, mix_norm[i], na_qkv_w[j], na_qkv_b[j], na_rpb[j], na_out_w[j], na_out_b[j], seqs, i)
        else:
            x = _mixer_ssd(x, mix_norm[i], ssm_in_w[j], ssm_conv_w[j], ssm_conv_b[j], ssm_dt_bias[j],
                           ssm_a_log[j], ssm_d[j], ssm_norm_w[j], ssm_out_w[j], starts, ends, i)
        x = mlp(x, mlp_norm[i], mlp_up_w[i].astype(BF16), mlp_down_w[i].astype(BF16), final_norm,
                final=(i == depth - 1), tm=1024, th=512, name=f"mlp_{i}")
    y_prompt = x[:bp * tp].reshape(bp, tp, d)
    y_sample = x[bp * tp:].reshape(bs, ts, d)
    return (y_prompt, y_sample)
```

```python
import functools

import numpy as np
import jax
import jax.numpy as jnp
from jax import lax
from jax.experimental import pallas as pl
from jax.experimental.pallas import tpu as pltpu

F32 = jnp.float32
BF16 = jnp.bfloat16

GRID_W = 64
NA_HEADS = 32
NA_HEAD_DIM = 64
NA_KH = 8
NA_KW = 16
SSM_HEAD_DIM = 64
SSM_GROUPS = 8
SSM_STATE = 128
SSM_CONV = 5
SSM_CHUNK = 128
N_MIXERS = 2
RMS_EPS = 1e-5

LANES = 128
SUBLANES = 8
VMEM_BYTES = 64 * 1024 * 1024
VMEM_CAP = VMEM_BYTES - 8 * 1024 * 1024


def _vmem_limit(pipelined_bytes, scratch_bytes=0, temp_bytes=0):
    need = 2 * pipelined_bytes + scratch_bytes + temp_bytes + (4 << 20)
    return int(min(max(need, 16 << 20), VMEM_CAP))


def _nbytes(shape, dtype):
    return int(np.prod(shape)) * jnp.dtype(dtype).itemsize


def _rms_scale(x):
    return x * lax.rsqrt(jnp.mean(x * x, axis=-1, keepdims=True) + RMS_EPS)


CAST_BLOCK_BYTES = 4 << 20


def _cast_kernel(w_ref, o_ref):
    o_ref[...] = w_ref[...].astype(o_ref.dtype)


def cast_bf16(w3, layer, *, name):
    _, k, n = w3.shape
    tk = k
    while tk % 2 == 0 and tk > 2 * SUBLANES and _nbytes((tk, n), F32) > CAST_BLOCK_BYTES:
        tk //= 2
    return pl.pallas_call(
        _cast_kernel,
        grid=(k // tk,),
        in_specs=[pl.BlockSpec((None, tk, n), lambda i: (layer, i, 0))],
        out_specs=pl.BlockSpec((tk, n), lambda i: (i, 0)),
        out_shape=jax.ShapeDtypeStruct((k, n), BF16),
        compiler_params=pltpu.CompilerParams(
            dimension_semantics=("parallel",),
            vmem_limit_bytes=_vmem_limit(_nbytes((tk, n), F32) + _nbytes((tk, n), BF16))),
        name=name,
    )(w3)


def _norm_matmul_kernel(x_ref, nw_ref, w_ref, *rest, has_bias, has_scale):
    rest = list(rest)
    b_ref = rest.pop(0) if has_bias else None
    s_ref = rest.pop(0) if has_scale else None
    o_ref, hn_ref = rest

    @pl.when(pl.program_id(1) == 0)
    def _():
        hn_ref[...] = (_rms_scale(x_ref[...]) * nw_ref[...]).astype(BF16)

    acc = jnp.dot(hn_ref[...], w_ref[...], preferred_element_type=F32)
    if has_bias:
        acc = acc + b_ref[...]
    if has_scale:
        acc = acc * s_ref[...]
    o_ref[...] = acc.astype(o_ref.dtype)


def norm_matmul(x, nw, w, out_dtype, *, b=None, scale=None, col0=0, n=None, tm, tn, name):
    t, d = x.shape
    n = w.shape[1] if n is None else n
    assert t % tm == 0 and n % tn == 0 and col0 % tn == 0
    jb0 = col0 // tn
    blocks = (_nbytes((tm, d), F32) + _nbytes((d, tn), BF16) + _nbytes((tm, tn), out_dtype))
    vec_spec = pl.BlockSpec((1, tn), lambda i, j: (0, j))
    extra = [v.reshape(1, n) for v in (b, scale) if v is not None]
    return pl.pallas_call(
        functools.partial(_norm_matmul_kernel, has_bias=b is not None, has_scale=scale is not None),
        grid=(t // tm, n // tn),
        in_specs=[
            pl.BlockSpec((tm, d), lambda i, j: (i, 0)),
            pl.BlockSpec((1, d), lambda i, j: (0, 0)),
            pl.BlockSpec((d, tn), lambda i, j: (0, jb0 + j)),
        ] + [vec_spec] * len(extra),
        out_specs=pl.BlockSpec((tm, tn), lambda i, j: (i, j)),
        out_shape=jax.ShapeDtypeStruct((t, n), out_dtype),
        scratch_shapes=[pltpu.VMEM((tm, d), BF16)],
        compiler_params=pltpu.CompilerParams(
            dimension_semantics=("parallel", "arbitrary"),
            vmem_limit_bytes=_vmem_limit(blocks, _nbytes((tm, d), BF16), _nbytes((tm, tn), F32))),
        name=name,
    )(x, nw.reshape(1, d), w, *extra)


def _matmul_res_kernel(a_ref, w_ref, r_ref, *rest, has_bias):
    acc = jnp.dot(a_ref[...], w_ref[...], preferred_element_type=F32)
    if has_bias:
        b_ref, o_ref = rest
        acc = acc + b_ref[...]
    else:
        (o_ref,) = rest
    o_ref[...] = r_ref[...] + acc


def matmul_res(a, w, res, *, b=None, tm, tn, name):
    t, k = a.shape
    n = w.shape[1]
    assert t % tm == 0 and n % tn == 0
    blocks = (_nbytes((tm, k), BF16) + _nbytes((k, tn), BF16) + 2 * _nbytes((tm, tn), F32))
    extra = [] if b is None else [b.reshape(1, n)]
    return pl.pallas_call(
        functools.partial(_matmul_res_kernel, has_bias=b is not None),
        grid=(t // tm, n // tn),
        in_specs=[
            pl.BlockSpec((tm, k), lambda i, j: (i, 0)),
            pl.BlockSpec((k, tn), lambda i, j: (0, j)),
            pl.BlockSpec((tm, tn), lambda i, j: (i, j)),
        ] + [pl.BlockSpec((1, tn), lambda i, j: (0, j))] * len(extra),
        out_specs=pl.BlockSpec((tm, tn), lambda i, j: (i, j)),
        out_shape=jax.ShapeDtypeStruct((t, n), F32),
        compiler_params=pltpu.CompilerParams(
            dimension_semantics=("parallel", "arbitrary"),
            vmem_limit_bytes=_vmem_limit(blocks, 0, _nbytes((tm, tn), F32))),
        name=name,
    )(a, w, res, *extra)


def _mlp_kernel(x_ref, nw_ref, wu_ref, wd_ref, fw_ref, o_ref, hn_ref, *, final):
    k = pl.program_id(1)

    @pl.when(k == 0)
    def _():
        x = x_ref[...]
        hn_ref[...] = (_rms_scale(x) * nw_ref[...]).astype(BF16)
        o_ref[...] = x

    h = jnp.dot(hn_ref[...], wu_ref[...], preferred_element_type=F32)
    h = jnp.square(jnp.maximum(h, 0.0)).astype(BF16)
    o_ref[...] += jnp.dot(h, wd_ref[...], preferred_element_type=F32)

    if final:
        @pl.when(k == pl.num_programs(1) - 1)
        def _():
            o_ref[...] = _rms_scale(o_ref[...]) * fw_ref[...]


def mlp(x, nw, wu, wd, fw, *, final, tm, th, name):
    t, d = x.shape
    hid = wu.shape[1]
    assert t % tm == 0 and hid % th == 0
    blocks = (2 * _nbytes((tm, d), F32) + _nbytes((d, th), BF16) + _nbytes((th, d), BF16))
    return pl.pallas_call(
        functools.partial(_mlp_kernel, final=final),
        grid=(t // tm, hid // th),
        in_specs=[
            pl.BlockSpec((tm, d), lambda i, k: (i, 0)),
            pl.BlockSpec((1, d), lambda i, k: (0, 0)),
            pl.BlockSpec((d, th), lambda i, k: (0, k)),
            pl.BlockSpec((th, d), lambda i, k: (k, 0)),
            pl.BlockSpec((1, d), lambda i, k: (0, 0)),
        ],
        out_specs=pl.BlockSpec((tm, d), lambda i, k: (i, 0)),
        out_shape=jax.ShapeDtypeStruct((t, d), F32),
        scratch_shapes=[pltpu.VMEM((tm, d), BF16)],
        compiler_params=pltpu.CompilerParams(
            dimension_semantics=("parallel", "arbitrary"),
            vmem_limit_bytes=_vmem_limit(blocks, _nbytes((tm, d), BF16),
                                         _nbytes((tm, th), F32) + _nbytes((tm, d), F32))),
        name=name,
    )(x, nw.reshape(1, d), wu, wd, fw.reshape(1, d))


LOG2_E = 1.4426950408889634
NA_WIN_TOK = NA_KH * GRID_W
NA_N_DR = 2 * NA_KH - 1
NA_ROWS_PER_ITER = 4


def _na_kernel(q_ref, k_ref, v_ref, rp_ref, o_ref, u_ref, s_ref, p_ref, l_ref, *, rows):
    W = GRID_W
    lane = lax.broadcasted_iota(jnp.int32, (W, LANES), 1)
    first_head = lane < NA_HEAD_DIM
    even_row = lane < W

    c = lax.broadcasted_iota(jnp.int32, (W, LANES), 0)
    kc = jnp.where(even_row, lane, lane - W)
    cs = jnp.clip(c - NA_KW // 2, 0, W - NA_KW)
    valid = (kc >= cs) & (kc < cs + NA_KW)
    shift = LANES - (NA_KW - 1)
    for h in range(2):
        t0_prev = None
        for dr in range(NA_N_DR):
            base = jnp.broadcast_to(rp_ref[h, dr:dr + 1, :] * LOG2_E, (W, LANES))
            t0 = pltpu.roll(base, shift, 1, stride=1, stride_axis=0)
            if dr >= 1:
                t1 = pltpu.roll(base, (shift + W) % LANES, 1, stride=1, stride_axis=0)
                u_ref[h, dr - 1] = jnp.where(valid, jnp.where(even_row, t0_prev, t1), -jnp.inf)
            t0_prev = t0

    RB = NA_ROWS_PER_ITER
    n_groups = rows // RB

    def win_start(r):
        return jnp.clip(r - NA_KH // 2, 0, rows - NA_KH)

    def scores(g):
        for u in range(RB):
            r = g * RB + u
            q0 = pl.multiple_of(r * W, W)
            k0 = pl.multiple_of(win_start(r) * W, W)
            q = q_ref[pl.ds(q0, W), :]
            zero = jnp.zeros_like(q)
            qs = jnp.concatenate([jnp.where(first_head, q, zero), jnp.where(first_head, zero, q)], axis=0)
            kk = k_ref[pl.ds(k0, NA_WIN_TOK), :]
            s_ref[u] = lax.dot_general(qs, kk, (((1,), (1,)), ((), ())), preferred_element_type=F32)

    def softmax(g):
        for u in range(RB):
            r = g * RB + u
            d0 = (NA_KH - 1) - (r - win_start(r))
            bias = jnp.concatenate(
                [jnp.concatenate([u_ref[h, d0 + 2 * j] for j in range(NA_KH // 2)], axis=1) for h in range(2)],
                axis=0)
            s = s_ref[u] + bias
            m = jnp.max(s, axis=-1, keepdims=True)
            p = jnp.exp2(s - m)
            p_ref[u] = p.astype(BF16)
            l_ref[u] = jnp.broadcast_to(jnp.sum(p, axis=-1, keepdims=True), (2 * W, LANES))

    def values(g):
        for u in range(RB):
            r = g * RB + u
            q0 = pl.multiple_of(r * W, W)
            k0 = pl.multiple_of(win_start(r) * W, W)
            vv = v_ref[pl.ds(k0, NA_WIN_TOK), :]
            pv = jnp.dot(p_ref[u], vv, preferred_element_type=F32) / l_ref[u]
            out = jnp.where(first_head, pv[:W], pv[W:])
            o_ref[pl.ds(q0, W), :] = out.astype(o_ref.dtype)

    scores(0)
    softmax(0)
    scores(1)

    def body(g, carry):
        values(g)
        softmax(g + 1)
        scores(g + 2)
        return carry

    lax.fori_loop(0, n_groups - 2, body, 0)
    values(n_groups - 2)
    softmax(n_groups - 1)
    values(n_groups - 1)


def na_core(qkv, rp, *, seq_len, seq_block0, n_seq, name):
    d = qkv.shape[1] // 3
    n_pairs = d // LANES
    rows = seq_len // GRID_W
    assert rows % NA_ROWS_PER_ITER == 0 and rows // NA_ROWS_PER_ITER >= 2 and rows >= NA_KH
    blk = (seq_len, LANES)
    u_shape = (2, NA_N_DR - 1, GRID_W, LANES)
    s_shape = (NA_ROWS_PER_ITER, 2 * GRID_W, NA_WIN_TOK)
    l_shape = (NA_ROWS_PER_ITER, 2 * GRID_W, LANES)
    scratch_bytes = (_nbytes(u_shape, F32) + _nbytes(s_shape, F32) + _nbytes(s_shape, BF16)
                     + _nbytes(l_shape, F32))
    blocks = 4 * _nbytes(blk, BF16) + _nbytes((2,) + rp.shape[1:], F32)
    return pl.pallas_call(
        functools.partial(_na_kernel, rows=rows),
        grid=(n_pairs, n_seq),
        in_specs=[
            pl.BlockSpec(blk, lambda h, b: (seq_block0 + b, h)),
            pl.BlockSpec(blk, lambda h, b: (seq_block0 + b, n_pairs + h)),
            pl.BlockSpec(blk, lambda h, b: (seq_block0 + b, 2 * n_pairs + h)),
            pl.BlockSpec((2,) + rp.shape[1:], lambda h, b: (h, 0, 0)),
        ],
        out_specs=pl.BlockSpec(blk, lambda h, b: (b, h)),
        out_shape=jax.ShapeDtypeStruct((n_seq * seq_len, d), BF16),
        scratch_shapes=[pltpu.VMEM(u_shape, F32), pltpu.VMEM(s_shape, F32), pltpu.VMEM(s_shape, BF16),
                        pltpu.VMEM(l_shape, F32)],
        compiler_params=pltpu.CompilerParams(
            dimension_semantics=("parallel", "arbitrary"),
            vmem_limit_bytes=_vmem_limit(blocks, scratch_bytes, 8 << 20)),
        name=name,
    )(qkv, qkv, qkv, rp)


def _na_pad_rpb(rpb):
    h, n_dr, n_dc = rpb.shape
    return jnp.pad(rpb.astype(F32), ((0, 0), (0, 2 * NA_KH - n_dr), (0, LANES - n_dc)))


def _split3(x):
    hi = x.astype(BF16)
    r1 = x - hi.astype(F32)
    mid = r1.astype(BF16)
    lo = (r1 - mid.astype(F32)).astype(BF16)
    return hi, mid, lo


def _dt_kernel(raw_ref, bias_ref, alog_ref, a_ref, ea_ref, at_ref, dtt_ref, wt_ref, *, n_chunks, n_heads):
    L = SSM_CHUNK
    x = raw_ref[...] + bias_ref[...]
    dt = jnp.maximum(x, 0.0) + jnp.log1p(jnp.exp(-jnp.abs(x)))
    ad = dt * (-jnp.exp(alog_ref[...]))
    row = lax.broadcasted_iota(jnp.int32, (L, L), 0)
    col = lax.broadcasted_iota(jnp.int32, (L, L), 1)
    tri_f = (col <= row).astype(BF16)
    tri_b = (col >= row).astype(BF16)
    fwd_lane = lax.broadcasted_iota(jnp.int32, (L, 2 * n_heads), 1) < n_heads
    for c in range(n_chunks):
        rows = slice(c * L, (c + 1) * L)
        parts = _split3(ad[rows])
        f = sum(jnp.dot(tri_f, p, preferred_element_type=F32) for p in parts)
        b = sum(jnp.dot(tri_b, p, preferred_element_type=F32) for p in parts)
        a = jnp.where(fwd_lane, f, b)
        a_end = jnp.where(fwd_lane[:1], f[L - 1:L], b[0:1])
        a2 = a * LOG2_E
        for part, (v2, v3) in enumerate(zip(_split3(a2), _split3(jnp.exp(a)))):
            a_ref[part, rows, :] = v2
            ea_ref[part, rows, :] = v3
        at_ref[c] = a2.T
        dtt_ref[c] = dt[rows].T
        wt_ref[c] = (dt[rows] * jnp.exp(a_end - a)).T


def ssd_steps(raw, dt_bias, a_log, *, tb, name):
    t, w = raw.shape
    n_chunks = tb // SSM_CHUNK
    assert t % tb == 0 and w == LANES
    tok_spec = pl.BlockSpec((tb, w), lambda i: (i, 0))
    hm_spec = pl.BlockSpec((n_chunks, w, SSM_CHUNK), lambda i: (i, 0, 0))
    hm_shape = jax.ShapeDtypeStruct((t // SSM_CHUNK, w, SSM_CHUNK), F32)
    split_spec = pl.BlockSpec((3, tb, w), lambda i: (0, i, 0))
    split_shape = jax.ShapeDtypeStruct((3, t, w), BF16)
    blocks = 3 * _nbytes((tb, w), F32) + 3 * _nbytes((n_chunks, w, SSM_CHUNK), F32)
    return pl.pallas_call(
        functools.partial(_dt_kernel, n_chunks=n_chunks, n_heads=w // 2),
        grid=(t // tb,),
        in_specs=[
            tok_spec,
            pl.BlockSpec((1, w), lambda i: (0, 0)),
            pl.BlockSpec((1, w), lambda i: (0, 0)),
        ],
        out_specs=[split_spec, split_spec, hm_spec, hm_spec, hm_spec],
        out_shape=[split_shape, split_shape, hm_shape, hm_shape, hm_shape],
        compiler_params=pltpu.CompilerParams(
            dimension_semantics=("parallel",),
            vmem_limit_bytes=_vmem_limit(blocks, 0, 4 << 20)),
        name=name,
    )(raw, dt_bias.reshape(1, w), a_log.reshape(1, w))


HALO = SUBLANES
SSD_SCAN_TB = 512


def _any_equal(v, values):
    return functools.reduce(jnp.logical_or, [v == s for s in values])


def _conv_kernel(prev_ref, main_ref, next_ref, w_ref, b_ref, o_ref, *, tb, starts, ends):
    start = pl.program_id(0) * tb
    at_start = _any_equal(start, starts)
    at_end = _any_equal(start + tb, ends)
    buf = jnp.concatenate([jnp.where(at_start, 0.0, prev_ref[...]), main_ref[...],
                           jnp.where(at_end, 0.0, next_ref[...])], axis=0)
    n = tb + 2 * HALO
    acc = b_ref[...] + main_ref[...] * w_ref[SSM_CONV // 2:SSM_CONV // 2 + 1, :]
    for k in range(SSM_CONV):
        if k != SSM_CONV // 2:
            rolled = pltpu.roll(buf, (SSM_CONV // 2 - k) % n, 0)
            acc = acc + rolled[HALO:HALO + tb] * w_ref[k:k + 1, :]
    o_ref[...] = (acc / (1.0 + jnp.exp(-acc))).astype(o_ref.dtype)


def ssd_conv(zx, conv_w, conv_b, *, col0, tb, tc, starts, ends, name):
    t = zx.shape[0]
    c = conv_w.shape[1]
    assert t % tb == 0 and c % tc == 0 and col0 % tc == 0 and tb % HALO == 0
    cb0 = col0 // tc
    hb = tb // HALO
    n_hb = t // HALO
    blocks = (_nbytes((tb, tc), F32) + _nbytes((tb, tc), BF16))
    return pl.pallas_call(
        functools.partial(_conv_kernel, tb=tb, starts=starts, ends=ends),
        grid=(t // tb, c // tc),
        in_specs=[
            pl.BlockSpec((HALO, tc), lambda i, j: (jnp.maximum(i * hb - 1, 0), cb0 + j)),
            pl.BlockSpec((tb, tc), lambda i, j: (i, cb0 + j)),
            pl.BlockSpec((HALO, tc), lambda i, j: (jnp.minimum((i + 1) * hb, n_hb - 1), cb0 + j)),
            pl.BlockSpec((SSM_CONV, tc), lambda i, j: (0, j)),
            pl.BlockSpec((1, tc), lambda i, j: (0, j)),
        ],
        out_specs=pl.BlockSpec((tb, tc), lambda i, j: (i, j)),
        out_shape=jax.ShapeDtypeStruct((t, c), BF16),
        compiler_params=pltpu.CompilerParams(
            dimension_semantics=("parallel", "arbitrary"),
            vmem_limit_bytes=_vmem_limit(blocks, 0, 8 * _nbytes((tb, tc), F32))),
        name=name,
    )(zx, zx, zx, conv_w, conv_b.reshape(1, c))


def _scan_kernel(x_ref, b_ref, c_ref, a3_ref, ea3_ref, at_ref, dtt_ref, wt_ref, *rest, tb, bounds, reverse):
    if reverse:
        yf_ref, z_ref, dsk_ref, nw_ref, o_ref, state_ref, y_ref = rest
    else:
        o_ref, state_ref = rest
        y_ref = o_ref
    L = SSM_CHUNK
    P = SSM_HEAD_DIM
    N = SSM_STATE
    n_chunks = tb // L
    heads = x_ref.shape[1] // P
    k3 = a3_ref.shape[1]
    def expansion(width):
        col_head = lax.broadcasted_iota(jnp.int32, (k3, heads * width), 0) % heads
        lane_head = lax.broadcasted_iota(jnp.int32, (k3, heads * width), 1) // width
        return jnp.where(col_head == lane_head, 1.0, 0.0).astype(BF16)

    exp_tile = expansion(L)
    exp_head = expansion(P)
    i = pl.program_id(1)
    blk = (pl.num_programs(1) - 1 - i) if reverse else i
    edge = (blk + 1) * tb if reverse else blk * tb

    @pl.when(_any_equal(edge, bounds))
    def _():
        state_ref[...] = jnp.zeros_like(state_ref)

    row = lax.broadcasted_iota(jnp.int32, (L, L), 0)
    col = lax.broadcasted_iota(jnp.int32, (L, L), 1)
    tri = (row <= col) if reverse else (row >= col)
    first_head = lax.broadcasted_iota(jnp.int32, (L, LANES), 1) < P
    first_head_n = lax.broadcasted_iota(jnp.int32, (N, LANES), 1) < P
    e_idx = 0 if reverse else L - 1

    def chunk(j, carry):
        c = (n_chunks - 1 - j) if reverse else j
        rows = pl.ds(pl.multiple_of(c * L, L), L)
        x = x_ref[rows, :]
        bm = b_ref[rows, :]
        cm = c_ref[rows, :]
        at = at_ref[c]
        dtt = dtt_ref[c]
        wt = wt_ref[c]
        a_col = jnp.dot(a3_ref[rows, :], exp_tile, preferred_element_type=F32)
        ea_x = jnp.dot(ea3_ref[rows, :], exp_head, preferred_element_type=F32)
        cb = lax.dot_general(cm, bm, (((1,), (1,)), ((), ())), preferred_element_type=F32)
        btf = bm.astype(F32).T
        y_off = jnp.dot(cm, state_ref[...].astype(BF16), preferred_element_type=F32) * ea_x
        for pair in range(heads // 2):
            cols = slice(pair * LANES, (pair + 1) * LANES)
            xp = x[:, cols]
            ys, bts, es = [], [], []
            for h in (2 * pair, 2 * pair + 1):
                seg = a_col[:, h * L:(h + 1) * L] - at[h:h + 1, :]
                m = (cb * dtt[h:h + 1, :]) * jnp.exp2(jnp.where(tri, seg, -jnp.inf))
                ys.append(jnp.dot(m.astype(BF16), xp, preferred_element_type=F32))
                bts.append((btf * wt[h:h + 1, :]).astype(BF16))
                es.append(jnp.exp2(at[h:h + 1, e_idx:e_idx + 1]))
            zs = jnp.dot(jnp.concatenate(bts, axis=0), xp, preferred_element_type=F32)
            z_pair = jnp.where(first_head_n, zs[:N], zs[N:])
            e_pair = jnp.where(first_head_n[:1], es[0], es[1])
            state_ref[:, cols] = state_ref[:, cols] * e_pair + z_pair
            y_pair = jnp.where(first_head, ys[0], ys[1]) + y_off[:, cols]
            if reverse:
                y_ref[:, cols] = y_pair
            else:
                y_ref[rows, cols] = y_pair

        if reverse:
            y = y_ref[...] + yf_ref[rows, :] + x.astype(F32) * dsk_ref[...]
            zz = z_ref[rows, :]
            y = y * (zz / (1.0 + jnp.exp(-zz)))
            o_ref[rows, :] = (_rms_scale(y) * nw_ref[...]).astype(o_ref.dtype)
        return carry

    lax.fori_loop(0, n_chunks, chunk, 0, unroll=True)


def ssd_scan(xc, a3g, ea3g, at, dtt, wt, extra, *, d_inner, tb, bounds, reverse, name):
    t = xc.shape[0]
    G = SSM_GROUPS
    gw = d_inner // G
    hpg = gw // SSM_HEAD_DIM
    n_blk = t // tb
    n_chunks = tb // SSM_CHUNK
    assert t % tb == 0 and gw % LANES == 0
    xb0 = d_inner // SSM_STATE
    cb0 = xb0 + G
    at_row0 = (at.shape[1] // 2 // hpg) if reverse else 0

    def tok(i):
        return (n_blk - 1 - i) if reverse else i

    tok_spec = pl.BlockSpec((None, tb, 3 * hpg), lambda g, i: (g, tok(i), 0))
    hm_spec = pl.BlockSpec((n_chunks, hpg, SSM_CHUNK), lambda g, i: (tok(i), at_row0 + g, 0))
    in_specs = [
        pl.BlockSpec((tb, gw), lambda g, i: (tok(i), g)),
        pl.BlockSpec((tb, SSM_STATE), lambda g, i: (tok(i), xb0 + g)),
        pl.BlockSpec((tb, SSM_STATE), lambda g, i: (tok(i), cb0 + g)),
        tok_spec, tok_spec, hm_spec, hm_spec, hm_spec,
    ]
    args = [xc, xc, xc, a3g, ea3g, at, dtt, wt]
    blocks = (_nbytes((tb, gw), BF16) + 2 * _nbytes((tb, SSM_STATE), BF16)
              + 2 * _nbytes((tb, LANES), BF16) + 3 * _nbytes((n_chunks, hpg, SSM_CHUNK), F32)
              + _nbytes((tb, gw), F32))
    if reverse:
        yf, zx, dsk, nw = extra
        in_specs += [
            pl.BlockSpec((tb, gw), lambda g, i: (tok(i), g)),
            pl.BlockSpec((tb, gw), lambda g, i: (tok(i), g)),
            pl.BlockSpec((1, gw), lambda g, i: (0, g)),
            pl.BlockSpec((1, gw), lambda g, i: (0, g)),
        ]
        args += [yf, zx, dsk, nw]
        blocks += 2 * _nbytes((tb, gw), F32)
    scratch = [pltpu.VMEM((SSM_STATE, gw), F32)]
    if reverse:
        scratch.append(pltpu.VMEM((SSM_CHUNK, gw), F32))
    return pl.pallas_call(
        functools.partial(_scan_kernel, tb=tb, bounds=bounds, reverse=reverse),
        grid=(G, n_blk),
        in_specs=in_specs,
        out_specs=pl.BlockSpec((tb, gw), lambda g, i: (tok(i), g)),
        out_shape=jax.ShapeDtypeStruct((t, d_inner), BF16 if reverse else F32),
        scratch_shapes=scratch,
        compiler_params=pltpu.CompilerParams(
            dimension_semantics=("parallel", "arbitrary"),
            vmem_limit_bytes=_vmem_limit(blocks, _nbytes((SSM_STATE, gw), F32), 8 << 20)),
        name=name,
    )(*args)


def _mixer_na(x, nw, qkv_w, qkv_b, rpb, out_w, out_b, seqs, li):
    d = x.shape[1]
    q_scale = jnp.concatenate([jnp.full((d,), NA_HEAD_DIM ** -0.5 * LOG2_E, F32), jnp.ones((2 * d,), F32)])
    qkv = norm_matmul(x, nw, qkv_w, BF16, b=qkv_b, scale=q_scale, tm=1024, tn=1024, name=f"na_qkv_{li}")
    rp = _na_pad_rpb(rpb)
    outs = []
    tok0 = 0
    for si, (seq_len, n_seq) in enumerate(seqs):
        assert tok0 % seq_len == 0
        outs.append(na_core(qkv, rp, seq_len=seq_len, seq_block0=tok0 // seq_len,
                            n_seq=n_seq, name=f"na_core_{li}_{si}"))
        tok0 += seq_len * n_seq
    o = jnp.concatenate(outs, axis=0)
    return matmul_res(o, out_w, x, b=out_b, tm=1024, tn=1024, name=f"na_out_{li}")


def _mixer_ssd(x, nw, in_w, conv_w, conv_b, dt_bias, a_log, d_skip, norm_w, out_w, starts, ends, li):
    d_inner = out_w.shape[0]
    conv_dim = conv_w.shape[1]
    n_main = d_inner + conv_dim
    n_dt = in_w.shape[1] - n_main
    heads = d_inner // SSM_HEAD_DIM
    hpg = heads // SSM_GROUPS
    t = x.shape[0]
    zx = norm_matmul(x, nw, in_w, F32, n=n_main, tm=1024, tn=1024, name=f"ssd_in_{li}")
    raw = norm_matmul(x, nw, in_w, F32, col0=n_main, n=n_dt, tm=1024, tn=n_dt, name=f"ssd_dt_{li}")
    a3, ea3, at, dtt, wt = ssd_steps(raw, dt_bias, a_log, tb=1024, name=f"ssd_steps_{li}")
    xc = ssd_conv(zx, conv_w, conv_b, col0=d_inner, tb=512, tc=512, starts=starts, ends=ends, name=f"ssd_conv_{li}")

    def per_group(v3):
        g = jnp.transpose(v3.reshape(3, t, 2, SSM_GROUPS, hpg), (2, 3, 1, 0, 4))
        return g.reshape(2, SSM_GROUPS, t, 3 * hpg)

    a3g = per_group(a3)
    ea3g = per_group(ea3)
    yf = ssd_scan(xc, a3g[0], ea3g[0], at, dtt, wt, None, d_inner=d_inner, tb=SSD_SCAN_TB, bounds=starts,
                  reverse=False, name=f"ssd_fwd_{li}")
    dsk = jnp.repeat(d_skip.astype(F32), SSM_HEAD_DIM).reshape(1, d_inner)
    y = ssd_scan(xc, a3g[1], ea3g[1], at, dtt, wt, (yf, zx, dsk, norm_w.reshape(1, d_inner)), d_inner=d_inner,
                 tb=SSD_SCAN_TB, bounds=ends, reverse=True, name=f"ssd_bwd_{li}")
    return matmul_res(y, out_w, x, tm=1024, tn=1024, name=f"ssd_out_{li}")


def kernel(x_prompt, x_sample, mix_norm, na_qkv_w, na_qkv_b, na_rpb, na_out_w, na_out_b, ssm_in_w, ssm_conv_w, ssm_conv_b, ssm_dt_bias, ssm_a_log, ssm_d, ssm_norm_w, ssm_out_w, mlp_norm, mlp_up_w, mlp_down_w, final_norm):
    bp, tp, d = x_prompt.shape
    bs, ts, _ = x_sample.shape
    depth = mix_norm.shape[0]
    x = jnp.concatenate([x_prompt.reshape(bp * tp, d), x_sample.reshape(bs * ts, d)], axis=0)
    seqs = ((tp, bp), (ts, bs))
    starts, ends = [], []
    tok = 0
    for seq_len, n_seq in seqs:
        for _ in range(n_seq):
            starts.append(tok)
            tok += seq_len
            ends.append(tok)
    starts, ends = tuple(starts), tuple(ends)

    for i in range(depth):
        j = i // N_MIXERS
        if i % N_MIXERS == 0:
            x = _mixer_na(x, mix_norm[i], cast_bf16(na_qkv_w, j, name=f"cast_qkv_{i}"), na_qkv_b[j], na_rpb[j],
                          cast_bf16(na_out_w, j, name=f"cast_na_out_{i}"), na_out_b[j], seqs, i)
        else:
            x = _mixer_ssd(x, mix_norm[i], cast_bf16(ssm_in_w, j, name=f"cast_ssd_in_{i}"), ssm_conv_w[j],
                           ssm_conv_b[j], ssm_dt_bias[j], ssm_a_log[j], ssm_d[j], ssm_norm_w[j],
                           cast_bf16(ssm_out_w, j, name=f"cast_ssd_out_{i}"), starts, ends, i)
        x = mlp(x, mlp_norm[i], cast_bf16(mlp_up_w, i, name=f"cast_up_{i}"),
                cast_bf16(mlp_down_w, i, name=f"cast_down_{i}"), final_norm,
                final=(i == depth - 1), tm=1024, th=512, name=f"mlp_{i}")
    y_prompt = x[:bp * tp].reshape(bp, tp, d)
    y_sample = x[bp * tp:].reshape(bs, ts, d)
    return (y_prompt, y_sample)
```

```python
import functools

import numpy as np
import jax
import jax.numpy as jnp
from jax import lax
from jax.experimental import pallas as pl
from jax.experimental.pallas import tpu as pltpu

F32 = jnp.float32
BF16 = jnp.bfloat16

GRID_W = 64
NA_HEADS = 32
NA_HEAD_DIM = 64
NA_KH = 8
NA_KW = 16
SSM_HEAD_DIM = 64
SSM_GROUPS = 8
SSM_STATE = 128
SSM_CONV = 5
SSM_CHUNK = 128
N_MIXERS = 2
RMS_EPS = 1e-5

LANES = 128
SUBLANES = 8
VMEM_BYTES = 64 * 1024 * 1024
VMEM_CAP = VMEM_BYTES - 8 * 1024 * 1024


def _vmem_limit(pipelined_bytes, scratch_bytes=0, temp_bytes=0):
    need = 2 * pipelined_bytes + scratch_bytes + temp_bytes + (4 << 20)
    return int(min(max(need, 16 << 20), VMEM_CAP))


def _nbytes(shape, dtype):
    return int(np.prod(shape)) * jnp.dtype(dtype).itemsize


def _rms_scale(x):
    return x * lax.rsqrt(jnp.mean(x * x, axis=-1, keepdims=True) + RMS_EPS)


CAST_BLOCK_BYTES = 4 << 20


def _cast_kernel(w_ref, o_ref):
    o_ref[...] = w_ref[...].astype(o_ref.dtype)


def cast_bf16(w3, layer, *, name):
    _, k, n = w3.shape
    tk = k
    while tk % 2 == 0 and tk > 2 * SUBLANES and _nbytes((tk, n), F32) > CAST_BLOCK_BYTES:
        tk //= 2
    return pl.pallas_call(
        _cast_kernel,
        grid=(k // tk,),
        in_specs=[pl.BlockSpec((None, tk, n), lambda i: (layer, i, 0))],
        out_specs=pl.BlockSpec((tk, n), lambda i: (i, 0)),
        out_shape=jax.ShapeDtypeStruct((k, n), BF16),
        compiler_params=pltpu.CompilerParams(
            dimension_semantics=("parallel",),
            vmem_limit_bytes=_vmem_limit(_nbytes((tk, n), F32) + _nbytes((tk, n), BF16))),
        name=name,
    )(w3)


def _norm_matmul_kernel(x_ref, nw_ref, w_ref, *rest, has_bias, has_scale):
    rest = list(rest)
    b_ref = rest.pop(0) if has_bias else None
    s_ref = rest.pop(0) if has_scale else None
    o_ref, hn_ref = rest

    @pl.when(pl.program_id(1) == 0)
    def _():
        hn_ref[...] = (_rms_scale(x_ref[...]) * nw_ref[...]).astype(BF16)

    acc = jnp.dot(hn_ref[...], w_ref[...], preferred_element_type=F32)
    if has_bias:
        acc = acc + b_ref[...]
    if has_scale:
        acc = acc * s_ref[...]
    o_ref[...] = acc.astype(o_ref.dtype)


def norm_matmul(x, nw, w, out_dtype, *, b=None, scale=None, col0=0, n=None, tm, tn, name):
    t, d = x.shape
    n = w.shape[1] if n is None else n
    assert t % tm == 0 and n % tn == 0 and col0 % tn == 0
    jb0 = col0 // tn
    blocks = (_nbytes((tm, d), F32) + _nbytes((d, tn), BF16) + _nbytes((tm, tn), out_dtype))
    vec_spec = pl.BlockSpec((1, tn), lambda i, j: (0, j))
    extra = [v.reshape(1, n) for v in (b, scale) if v is not None]
    return pl.pallas_call(
        functools.partial(_norm_matmul_kernel, has_bias=b is not None, has_scale=scale is not None),
        grid=(t // tm, n // tn),
        in_specs=[
            pl.BlockSpec((tm, d), lambda i, j: (i, 0)),
            pl.BlockSpec((1, d), lambda i, j: (0, 0)),
            pl.BlockSpec((d, tn), lambda i, j: (0, jb0 + j)),
        ] + [vec_spec] * len(extra),
        out_specs=pl.BlockSpec((tm, tn), lambda i, j: (i, j)),
        out_shape=jax.ShapeDtypeStruct((t, n), out_dtype),
        scratch_shapes=[pltpu.VMEM((tm, d), BF16)],
        compiler_params=pltpu.CompilerParams(
            dimension_semantics=("parallel", "arbitrary"),
            vmem_limit_bytes=_vmem_limit(blocks, _nbytes((tm, d), BF16), _nbytes((tm, tn), F32))),
        name=name,
    )(x, nw.reshape(1, d), w, *extra)


def _matmul_res_kernel(a_ref, w_ref, r_ref, *rest, has_bias):
    acc = jnp.dot(a_ref[...], w_ref[...], preferred_element_type=F32)
    if has_bias:
        b_ref, o_ref = rest
        acc = acc + b_ref[...]
    else:
        (o_ref,) = rest
    o_ref[...] = r_ref[...] + acc


def matmul_res(a, w, res, *, b=None, tm, tn, name):
    t, k = a.shape
    n = w.shape[1]
    assert t % tm == 0 and n % tn == 0
    blocks = (_nbytes((tm, k), BF16) + _nbytes((k, tn), BF16) + 2 * _nbytes((tm, tn), F32))
    extra = [] if b is None else [b.reshape(1, n)]
    return pl.pallas_call(
        functools.partial(_matmul_res_kernel, has_bias=b is not None),
        grid=(t // tm, n // tn),
        in_specs=[
            pl.BlockSpec((tm, k), lambda i, j: (i, 0)),
            pl.BlockSpec((k, tn), lambda i, j: (0, j)),
            pl.BlockSpec((tm, tn), lambda i, j: (i, j)),
        ] + [pl.BlockSpec((1, tn), lambda i, j: (0, j))] * len(extra),
        out_specs=pl.BlockSpec((tm, tn), lambda i, j: (i, j)),
        out_shape=jax.ShapeDtypeStruct((t, n), F32),
        compiler_params=pltpu.CompilerParams(
            dimension_semantics=("parallel", "arbitrary"),
            vmem_limit_bytes=_vmem_limit(blocks, 0, _nbytes((tm, tn), F32))),
        name=name,
    )(a, w, res, *extra)


def _mlp_kernel(x_ref, nw_ref, wu_ref, wd_ref, fw_ref, o_ref, hn_ref, *, final):
    k = pl.program_id(1)

    @pl.when(k == 0)
    def _():
        x = x_ref[...]
        hn_ref[...] = (_rms_scale(x) * nw_ref[...]).astype(BF16)
        o_ref[...] = x

    h = jnp.dot(hn_ref[...], wu_ref[...], preferred_element_type=F32)
    h = jnp.square(jnp.maximum(h, 0.0)).astype(BF16)
    o_ref[...] += jnp.dot(h, wd_ref[...], preferred_element_type=F32)

    if final:
        @pl.when(k == pl.num_programs(1) - 1)
        def _():
            o_ref[...] = _rms_scale(o_ref[...]) * fw_ref[...]


def mlp(x, nw, wu, wd, fw, *, final, tm, th, name):
    t, d = x.shape
    hid = wu.shape[1]
    assert t % tm == 0 and hid % th == 0
    blocks = (2 * _nbytes((tm, d), F32) + _nbytes((d, th), BF16) + _nbytes((th, d), BF16))
    return pl.pallas_call(
        functools.partial(_mlp_kernel, final=final),
        grid=(t // tm, hid // th),
        in_specs=[
            pl.BlockSpec((tm, d), lambda i, k: (i, 0)),
            pl.BlockSpec((1, d), lambda i, k: (0, 0)),
            pl.BlockSpec((d, th), lambda i, k: (0, k)),
            pl.BlockSpec((th, d), lambda i, k: (k, 0)),
            pl.BlockSpec((1, d), lambda i, k: (0, 0)),
        ],
        out_specs=pl.BlockSpec((tm, d), lambda i, k: (i, 0)),
        out_shape=jax.ShapeDtypeStruct((t, d), F32),
        scratch_shapes=[pltpu.VMEM((tm, d), BF16)],
        compiler_params=pltpu.CompilerParams(
            dimension_semantics=("parallel", "arbitrary"),
            vmem_limit_bytes=_vmem_limit(blocks, _nbytes((tm, d), BF16),
                                         _nbytes((tm, th), F32) + _nbytes((tm, d), F32))),
        name=name,
    )(x, nw.reshape(1, d), wu, wd, fw.reshape(1, d))


LOG2_E = 1.4426950408889634
NA_WIN_TOK = NA_KH * GRID_W
NA_N_DR = 2 * NA_KH - 1
NA_ROWS_PER_ITER = 8


def _na_kernel(q_ref, k_ref, v_ref, rp_ref, o_ref, u_ref, s_ref, p_ref, l_ref, *, rows):
    W = GRID_W
    lane = lax.broadcasted_iota(jnp.int32, (W, LANES), 1)
    first_head = lane < NA_HEAD_DIM
    even_row = lane < W

    c = lax.broadcasted_iota(jnp.int32, (W, LANES), 0)
    kc = jnp.where(even_row, lane, lane - W)
    cs = jnp.clip(c - NA_KW // 2, 0, W - NA_KW)
    valid = (kc >= cs) & (kc < cs + NA_KW)
    shift = LANES - (NA_KW - 1)
    for h in range(2):
        t0_prev = None
        for dr in range(NA_N_DR):
            base = jnp.broadcast_to(rp_ref[h, dr:dr + 1, :] * LOG2_E, (W, LANES))
            t0 = pltpu.roll(base, shift, 1, stride=1, stride_axis=0)
            if dr >= 1:
                t1 = pltpu.roll(base, (shift + W) % LANES, 1, stride=1, stride_axis=0)
                u_ref[h, dr - 1] = jnp.where(valid, jnp.where(even_row, t0_prev, t1), -jnp.inf)
            t0_prev = t0

    RB = NA_ROWS_PER_ITER
    n_groups = rows // RB

    def win_start(r):
        return jnp.clip(r - NA_KH // 2, 0, rows - NA_KH)

    def scores(g):
        for u in range(RB):
            r = g * RB + u
            q0 = pl.multiple_of(r * W, W)
            k0 = pl.multiple_of(win_start(r) * W, W)
            q = q_ref[pl.ds(q0, W), :]
            zero = jnp.zeros_like(q)
            qs = jnp.concatenate([jnp.where(first_head, q, zero), jnp.where(first_head, zero, q)], axis=0)
            kk = k_ref[pl.ds(k0, NA_WIN_TOK), :]
            s_ref[u] = lax.dot_general(qs, kk, (((1,), (1,)), ((), ())), preferred_element_type=F32)

    def softmax(g):
        for u in range(RB):
            r = g * RB + u
            d0 = (NA_KH - 1) - (r - win_start(r))
            bias = jnp.concatenate(
                [jnp.concatenate([u_ref[h, d0 + 2 * j] for j in range(NA_KH // 2)], axis=1) for h in range(2)],
                axis=0)
            s = s_ref[u] + bias
            m = jnp.max(s, axis=-1, keepdims=True)
            p = jnp.exp2(s - m)
            p_ref[u] = p.astype(BF16)
            l_ref[u] = jnp.broadcast_to(jnp.sum(p, axis=-1, keepdims=True), (2 * W, LANES))

    def values(g):
        for u in range(RB):
            r = g * RB + u
            q0 = pl.multiple_of(r * W, W)
            k0 = pl.multiple_of(win_start(r) * W, W)
            vv = v_ref[pl.ds(k0, NA_WIN_TOK), :]
            pv = jnp.dot(p_ref[u], vv, preferred_element_type=F32) / l_ref[u]
            out = jnp.where(first_head, pv[:W], pv[W:])
            o_ref[pl.ds(q0, W), :] = out.astype(o_ref.dtype)

    scores(0)
    softmax(0)
    scores(1)

    def body(g, carry):
        values(g)
        softmax(g + 1)
        scores(g + 2)
        return carry

    lax.fori_loop(0, n_groups - 2, body, 0)
    values(n_groups - 2)
    softmax(n_groups - 1)
    values(n_groups - 1)


def na_core(qkv, rp, *, seq_len, seq_block0, n_seq, name):
    d = qkv.shape[1] // 3
    n_pairs = d // LANES
    rows = seq_len // GRID_W
    assert rows % NA_ROWS_PER_ITER == 0 and rows // NA_ROWS_PER_ITER >= 2 and rows >= NA_KH
    blk = (seq_len, LANES)
    u_shape = (2, NA_N_DR - 1, GRID_W, LANES)
    s_shape = (NA_ROWS_PER_ITER, 2 * GRID_W, NA_WIN_TOK)
    l_shape = (NA_ROWS_PER_ITER, 2 * GRID_W, LANES)
    scratch_bytes = (_nbytes(u_shape, F32) + _nbytes(s_shape, F32) + _nbytes(s_shape, BF16)
                     + _nbytes(l_shape, F32))
    blocks = 4 * _nbytes(blk, BF16) + _nbytes((2,) + rp.shape[1:], F32)
    return pl.pallas_call(
        functools.partial(_na_kernel, rows=rows),
        grid=(n_pairs, n_seq),
        in_specs=[
            pl.BlockSpec(blk, lambda h, b: (seq_block0 + b, h)),
            pl.BlockSpec(blk, lambda h, b: (seq_block0 + b, n_pairs + h)),
            pl.BlockSpec(blk, lambda h, b: (seq_block0 + b, 2 * n_pairs + h)),
            pl.BlockSpec((2,) + rp.shape[1:], lambda h, b: (h, 0, 0)),
        ],
        out_specs=pl.BlockSpec(blk, lambda h, b: (b, h)),
        out_shape=jax.ShapeDtypeStruct((n_seq * seq_len, d), BF16),
        scratch_shapes=[pltpu.VMEM(u_shape, F32), pltpu.VMEM(s_shape, F32), pltpu.VMEM(s_shape, BF16),
                        pltpu.VMEM(l_shape, F32)],
        compiler_params=pltpu.CompilerParams(
            dimension_semantics=("parallel", "arbitrary"),
            vmem_limit_bytes=_vmem_limit(blocks, scratch_bytes, 8 << 20)),
        name=name,
    )(qkv, qkv, qkv, rp)


def _na_pad_rpb(rpb):
    h, n_dr, n_dc = rpb.shape
    return jnp.pad(rpb.astype(F32), ((0, 0), (0, 2 * NA_KH - n_dr), (0, LANES - n_dc)))


def _split3(x):
    hi = x.astype(BF16)
    r1 = x - hi.astype(F32)
    mid = r1.astype(BF16)
    lo = (r1 - mid.astype(F32)).astype(BF16)
    return hi, mid, lo


def _dt_kernel(raw_ref, bias_ref, alog_ref, a_ref, ea_ref, w_ref, atd_ref, eend_ref, *, n_chunks, n_heads):
    L = SSM_CHUNK
    x = raw_ref[...] + bias_ref[...]
    dt = jnp.maximum(x, 0.0) + jnp.log1p(jnp.exp(-jnp.abs(x)))
    log2_dt = jnp.log(dt) * LOG2_E
    ad = dt * (-jnp.exp(alog_ref[...]))
    row = lax.broadcasted_iota(jnp.int32, (L, L), 0)
    col = lax.broadcasted_iota(jnp.int32, (L, L), 1)
    tri_f = (col <= row).astype(BF16)
    tri_b = (col >= row).astype(BF16)
    fwd_lane = lax.broadcasted_iota(jnp.int32, (L, 2 * n_heads), 1) < n_heads
    for c in range(n_chunks):
        rows = slice(c * L, (c + 1) * L)
        parts = _split3(ad[rows])
        f = sum(jnp.dot(tri_f, p, preferred_element_type=F32) for p in parts)
        b = sum(jnp.dot(tri_b, p, preferred_element_type=F32) for p in parts)
        a = jnp.where(fwd_lane, f, b)
        a_end = jnp.where(fwd_lane[:1], f[L - 1:L], b[0:1])
        a2 = a * LOG2_E
        w = dt[rows] * jnp.exp(a_end - a)
        a_ref[rows, :] = a2
        for part, (ve, vw) in enumerate(zip(_split3(jnp.exp(a)), _split3(w))):
            ea_ref[part, rows, :] = ve
            w_ref[part, rows, :] = vw
        atd_ref[c] = (a2 - log2_dt[rows]).T
        eend_ref[c] = jnp.broadcast_to(jnp.exp(a_end), a.shape).T


def ssd_steps(raw, dt_bias, a_log, *, tb, name):
    t, w = raw.shape
    n_chunks = tb // SSM_CHUNK
    assert t % tb == 0 and w == LANES
    tok_spec = pl.BlockSpec((tb, w), lambda i: (i, 0))
    hm_spec = pl.BlockSpec((n_chunks, w, SSM_CHUNK), lambda i: (i, 0, 0))
    hm_shape = jax.ShapeDtypeStruct((t // SSM_CHUNK, w, SSM_CHUNK), F32)
    split_spec = pl.BlockSpec((3, tb, w), lambda i: (0, i, 0))
    split_shape = jax.ShapeDtypeStruct((3, t, w), BF16)
    blocks = 3 * _nbytes((tb, w), F32) + 3 * _nbytes((n_chunks, w, SSM_CHUNK), F32)
    return pl.pallas_call(
        functools.partial(_dt_kernel, n_chunks=n_chunks, n_heads=w // 2),
        grid=(t // tb,),
        in_specs=[
            tok_spec,
            pl.BlockSpec((1, w), lambda i: (0, 0)),
            pl.BlockSpec((1, w), lambda i: (0, 0)),
        ],
        out_specs=[tok_spec, split_spec, split_spec, hm_spec, hm_spec],
        out_shape=[jax.ShapeDtypeStruct((t, w), F32), split_shape, split_shape, hm_shape, hm_shape],
        compiler_params=pltpu.CompilerParams(
            dimension_semantics=("parallel",),
            vmem_limit_bytes=_vmem_limit(blocks, 0, 4 << 20)),
        name=name,
    )(raw, dt_bias.reshape(1, w), a_log.reshape(1, w))


HALO = SUBLANES
SSD_SCAN_TB = 512


def _any_equal(v, values):
    return functools.reduce(jnp.logical_or, [v == s for s in values])


def _conv_kernel(prev_ref, main_ref, next_ref, w_ref, b_ref, o_ref, *, tb, starts, ends):
    start = pl.program_id(0) * tb
    at_start = _any_equal(start, starts)
    at_end = _any_equal(start + tb, ends)
    buf = jnp.concatenate([jnp.where(at_start, 0.0, prev_ref[...]), main_ref[...],
                           jnp.where(at_end, 0.0, next_ref[...])], axis=0)
    n = tb + 2 * HALO
    acc = b_ref[...] + main_ref[...] * w_ref[SSM_CONV // 2:SSM_CONV // 2 + 1, :]
    for k in range(SSM_CONV):
        if k != SSM_CONV // 2:
            rolled = pltpu.roll(buf, (SSM_CONV // 2 - k) % n, 0)
            acc = acc + rolled[HALO:HALO + tb] * w_ref[k:k + 1, :]
    o_ref[...] = (acc / (1.0 + jnp.exp(-acc))).astype(o_ref.dtype)


def ssd_conv(zx, conv_w, conv_b, *, col0, tb, tc, starts, ends, name):
    t = zx.shape[0]
    c = conv_w.shape[1]
    assert t % tb == 0 and c % tc == 0 and col0 % tc == 0 and tb % HALO == 0
    cb0 = col0 // tc
    hb = tb // HALO
    n_hb = t // HALO
    blocks = (_nbytes((tb, tc), F32) + _nbytes((tb, tc), BF16))
    return pl.pallas_call(
        functools.partial(_conv_kernel, tb=tb, starts=starts, ends=ends),
        grid=(t // tb, c // tc),
        in_specs=[
            pl.BlockSpec((HALO, tc), lambda i, j: (jnp.maximum(i * hb - 1, 0), cb0 + j)),
            pl.BlockSpec((tb, tc), lambda i, j: (i, cb0 + j)),
            pl.BlockSpec((HALO, tc), lambda i, j: (jnp.minimum((i + 1) * hb, n_hb - 1), cb0 + j)),
            pl.BlockSpec((SSM_CONV, tc), lambda i, j: (0, j)),
            pl.BlockSpec((1, tc), lambda i, j: (0, j)),
        ],
        out_specs=pl.BlockSpec((tb, tc), lambda i, j: (i, j)),
        out_shape=jax.ShapeDtypeStruct((t, c), BF16),
        compiler_params=pltpu.CompilerParams(
            dimension_semantics=("parallel", "arbitrary"),
            vmem_limit_bytes=_vmem_limit(blocks, 0, 8 * _nbytes((tb, tc), F32))),
        name=name,
    )(zx, zx, zx, conv_w, conv_b.reshape(1, c))


def _scan_kernel(x_ref, b_ref, c_ref, a_ref, ea3_ref, w3_ref, atd_ref, eend_ref, *rest, tb, bounds, reverse):
    if reverse:
        yf_ref, z_ref, dsk_ref, nw_ref, o_ref, state_ref, y_ref = rest
    else:
        o_ref, state_ref = rest
        y_ref = o_ref
    L = SSM_CHUNK
    P = SSM_HEAD_DIM
    N = SSM_STATE
    n_chunks = tb // L
    heads = x_ref.shape[1] // P
    k3 = ea3_ref.shape[1]
    col_head = lax.broadcasted_iota(jnp.int32, (k3, heads * P), 0) % heads
    lane_head = lax.broadcasted_iota(jnp.int32, (k3, heads * P), 1) // P
    exp_head = jnp.where(col_head == lane_head, 1.0, 0.0).astype(BF16)
    i = pl.program_id(1)
    blk = (pl.num_programs(1) - 1 - i) if reverse else i
    edge = (blk + 1) * tb if reverse else blk * tb

    @pl.when(_any_equal(edge, bounds))
    def _():
        state_ref[...] = jnp.zeros_like(state_ref)

    row = lax.broadcasted_iota(jnp.int32, (L, L), 0)
    col = lax.broadcasted_iota(jnp.int32, (L, L), 1)
    tri = (row <= col) if reverse else (row >= col)
    first_head = lax.broadcasted_iota(jnp.int32, (L, LANES), 1) < P
    first_head_n = lax.broadcasted_iota(jnp.int32, (N, LANES), 1) < P

    def chunk(j, carry):
        c = (n_chunks - 1 - j) if reverse else j
        rows = pl.ds(pl.multiple_of(c * L, L), L)
        x = x_ref[rows, :]
        bm = b_ref[rows, :]
        cm = c_ref[rows, :]
        atd = atd_ref[c]
        eend = eend_ref[c]
        a = a_ref[rows, :]
        ea_x = jnp.dot(ea3_ref[rows, :], exp_head, preferred_element_type=F32)
        w_x = jnp.dot(w3_ref[rows, :], exp_head, preferred_element_type=F32)
        cb = lax.dot_general(cm, bm, (((1,), (1,)), ((), ())), preferred_element_type=F32)
        y_off = jnp.dot(cm, state_ref[...].astype(BF16), preferred_element_type=F32) * ea_x
        xw = (x.astype(F32) * w_x).astype(BF16)
        z_new = lax.dot_general(bm, xw, (((0,), (0,)), ((), ())), preferred_element_type=F32)
        for pair in range(heads // 2):
            cols = slice(pair * LANES, (pair + 1) * LANES)
            xp = x[:, cols]
            ys = []
            for h in (2 * pair, 2 * pair + 1):
                seg = a[:, h:h + 1] - atd[h:h + 1, :]
                m = cb * jnp.exp2(jnp.where(tri, seg, -jnp.inf))
                ys.append(jnp.dot(m.astype(BF16), xp, preferred_element_type=F32))
            e_pair = jnp.where(first_head_n[:1], eend[2 * pair:2 * pair + 1, :], eend[2 * pair + 1:2 * pair + 2, :])
            state_ref[:, cols] = state_ref[:, cols] * e_pair + z_new[:, cols]
            y_pair = jnp.where(first_head, ys[0], ys[1]) + y_off[:, cols]
            if reverse:
                y_ref[:, cols] = y_pair
            else:
                y_ref[rows, cols] = y_pair

        if reverse:
            y = y_ref[...] + yf_ref[rows, :] + x.astype(F32) * dsk_ref[...]
            zz = z_ref[rows, :]
            y = y * (zz / (1.0 + jnp.exp(-zz)))
            o_ref[rows, :] = (_rms_scale(y) * nw_ref[...]).astype(o_ref.dtype)
        return carry

    lax.fori_loop(0, n_chunks, chunk, 0, unroll=True)


def ssd_scan(xc, ag, ea3g, w3g, atd, eend, extra, *, d_inner, tb, bounds, reverse, name):
    t = xc.shape[0]
    G = SSM_GROUPS
    gw = d_inner // G
    hpg = gw // SSM_HEAD_DIM
    n_blk = t // tb
    n_chunks = tb // SSM_CHUNK
    assert t % tb == 0 and gw % LANES == 0
    xb0 = d_inner // SSM_STATE
    cb0 = xb0 + G
    at_row0 = (atd.shape[1] // 2 // hpg) if reverse else 0

    def tok(i):
        return (n_blk - 1 - i) if reverse else i

    tok_spec = pl.BlockSpec((None, tb, hpg), lambda g, i: (g, tok(i), 0))
    tok3_spec = pl.BlockSpec((None, tb, 3 * hpg), lambda g, i: (g, tok(i), 0))
    hm_spec = pl.BlockSpec((n_chunks, hpg, SSM_CHUNK), lambda g, i: (tok(i), at_row0 + g, 0))
    in_specs = [
        pl.BlockSpec((tb, gw), lambda g, i: (tok(i), g)),
        pl.BlockSpec((tb, SSM_STATE), lambda g, i: (tok(i), xb0 + g)),
        pl.BlockSpec((tb, SSM_STATE), lambda g, i: (tok(i), cb0 + g)),
        tok_spec, tok3_spec, tok3_spec, hm_spec, hm_spec,
    ]
    args = [xc, xc, xc, ag, ea3g, w3g, atd, eend]
    blocks = (_nbytes((tb, gw), BF16) + 2 * _nbytes((tb, SSM_STATE), BF16)
              + 3 * _nbytes((tb, LANES), BF16) + 2 * _nbytes((n_chunks, hpg, SSM_CHUNK), F32)
              + _nbytes((tb, gw), F32))
    if reverse:
        yf, zx, dsk, nw = extra
        in_specs += [
            pl.BlockSpec((tb, gw), lambda g, i: (tok(i), g)),
            pl.BlockSpec((tb, gw), lambda g, i: (tok(i), g)),
            pl.BlockSpec((1, gw), lambda g, i: (0, g)),
            pl.BlockSpec((1, gw), lambda g, i: (0, g)),
        ]
        args += [yf, zx, dsk, nw]
        blocks += 2 * _nbytes((tb, gw), F32)
    scratch = [pltpu.VMEM((SSM_STATE, gw), F32)]
    if reverse:
        scratch.append(pltpu.VMEM((SSM_CHUNK, gw), F32))
    return pl.pallas_call(
        functools.partial(_scan_kernel, tb=tb, bounds=bounds, reverse=reverse),
        grid=(G, n_blk),
        in_specs=in_specs,
        out_specs=pl.BlockSpec((tb, gw), lambda g, i: (tok(i), g)),
        out_shape=jax.ShapeDtypeStruct((t, d_inner), BF16 if reverse else F32),
        scratch_shapes=scratch,
        compiler_params=pltpu.CompilerParams(
            dimension_semantics=("parallel", "arbitrary"),
            vmem_limit_bytes=_vmem_limit(blocks, _nbytes((SSM_STATE, gw), F32), 8 << 20)),
        name=name,
    )(*args)


def _mixer_na(x, nw, qkv_w, qkv_b, rpb, out_w, out_b, seqs, li):
    d = x.shape[1]
    q_scale = jnp.concatenate([jnp.full((d,), NA_HEAD_DIM ** -0.5 * LOG2_E, F32), jnp.ones((2 * d,), F32)])
    qkv = norm_matmul(x, nw, qkv_w, BF16, b=qkv_b, scale=q_scale, tm=1024, tn=1024, name=f"na_qkv_{li}")
    rp = _na_pad_rpb(rpb)
    outs = []
    tok0 = 0
    for si, (seq_len, n_seq) in enumerate(seqs):
        assert tok0 % seq_len == 0
        outs.append(na_core(qkv, rp, seq_len=seq_len, seq_block0=tok0 // seq_len,
                            n_seq=n_seq, name=f"na_core_{li}_{si}"))
        tok0 += seq_len * n_seq
    o = jnp.concatenate(outs, axis=0)
    return matmul_res(o, out_w, x, b=out_b, tm=1024, tn=1024, name=f"na_out_{li}")


def _mixer_ssd(x, nw, in_w, conv_w, conv_b, dt_bias, a_log, d_skip, norm_w, out_w, starts, ends, li):
    d_inner = out_w.shape[0]
    conv_dim = conv_w.shape[1]
    n_main = d_inner + conv_dim
    n_dt = in_w.shape[1] - n_main
    heads = d_inner // SSM_HEAD_DIM
    hpg = heads // SSM_GROUPS
    t = x.shape[0]
    zx = norm_matmul(x, nw, in_w, F32, n=n_main, tm=1024, tn=1024, name=f"ssd_in_{li}")
    raw = norm_matmul(x, nw, in_w, F32, col0=n_main, n=n_dt, tm=1024, tn=n_dt, name=f"ssd_dt_{li}")
    a, ea3, w3, atd, eend = ssd_steps(raw, dt_bias, a_log, tb=1024, name=f"ssd_steps_{li}")
    xc = ssd_conv(zx, conv_w, conv_b, col0=d_inner, tb=512, tc=512, starts=starts, ends=ends, name=f"ssd_conv_{li}")

    def per_group(v):
        parts = v.shape[0]
        g = jnp.transpose(v.reshape(parts, t, 2, SSM_GROUPS, hpg), (2, 3, 1, 0, 4))
        return g.reshape(2, SSM_GROUPS, t, parts * hpg)

    ag = per_group(a[None])
    ea3g = per_group(ea3)
    w3g = per_group(w3)
    yf = ssd_scan(xc, ag[0], ea3g[0], w3g[0], atd, eend, None, d_inner=d_inner, tb=SSD_SCAN_TB,
                  bounds=starts, reverse=False, name=f"ssd_fwd_{li}")
    dsk = jnp.repeat(d_skip.astype(F32), SSM_HEAD_DIM).reshape(1, d_inner)
    y = ssd_scan(xc, ag[1], ea3g[1], w3g[1], atd, eend, (yf, zx, dsk, norm_w.reshape(1, d_inner)),
                 d_inner=d_inner, tb=SSD_SCAN_TB, bounds=ends, reverse=True, name=f"ssd_bwd_{li}")
    return matmul_res(y, out_w, x, tm=1024, tn=1024, name=f"ssd_out_{li}")


def kernel(x_prompt, x_sample, mix_norm, na_qkv_w, na_qkv_b, na_rpb, na_out_w, na_out_b, ssm_in_w, ssm_conv_w, ssm_conv_b, ssm_dt_bias, ssm_a_log, ssm_d, ssm_norm_w, ssm_out_w, mlp_norm, mlp_up_w, mlp_down_w, final_norm):
    bp, tp, d = x_prompt.shape
    bs, ts, _ = x_sample.shape
    depth = mix_norm.shape[0]
    x = jnp.concatenate([x_prompt.reshape(bp * tp, d), x_sample.reshape(bs * ts, d)], axis=0)
    seqs = ((tp, bp), (ts, bs))
    starts, ends = [], []
    tok = 0
    for seq_len, n_seq in seqs:
        for _ in range(n_seq):
            starts.append(tok)
            tok += seq_len
            ends.append(tok)
    starts, ends = tuple(starts), tuple(ends)

    for i in range(depth):
        j = i // N_MIXERS
        if i % N_MIXERS == 0:
            x = _mixer_na(x, mix_norm[i], cast_bf16(na_qkv_w, j, name=f"cast_qkv_{i}"), na_qkv_b[j], na_rpb[j],
                          cast_bf16(na_out_w, j, name=f"cast_na_out_{i}"), na_out_b[j], seqs, i)
        else:
            x = _mixer_ssd(x, mix_norm[i], cast_bf16(ssm_in_w, j, name=f"cast_ssd_in_{i}"), ssm_conv_w[j],
                           ssm_conv_b[j], ssm_dt_bias[j], ssm_a_log[j], ssm_d[j], ssm_norm_w[j],
                           cast_bf16(ssm_out_w, j, name=f"cast_ssd_out_{i}"), starts, ends, i)
        x = mlp(x, mlp_norm[i], cast_bf16(mlp_up_w, i, name=f"cast_up_{i}"),
                cast_bf16(mlp_down_w, i, name=f"cast_down_{i}"), final_norm,
                final=(i == depth - 1), tm=1024, th=512, name=f"mlp_{i}")
    y_prompt = x[:bp * tp].reshape(bp, tp, d)
    y_sample = x[bp * tp:].reshape(bs, ts, d)
    return (y_prompt, y_sample)
```

```python
import functools

import numpy as np
import jax
import jax.numpy as jnp
from jax import lax
from jax.experimental import pallas as pl
from jax.experimental.pallas import tpu as pltpu

F32 = jnp.float32
BF16 = jnp.bfloat16

GRID_W = 64
NA_HEADS = 32
NA_HEAD_DIM = 64
NA_KH = 8
NA_KW = 16
SSM_HEAD_DIM = 64
SSM_GROUPS = 8
SSM_STATE = 128
SSM_CONV = 5
SSM_CHUNK = 128
N_MIXERS = 2
RMS_EPS = 1e-5

LANES = 128
SUBLANES = 8
VMEM_BYTES = 64 * 1024 * 1024
VMEM_CAP = VMEM_BYTES - 8 * 1024 * 1024


def _vmem_limit(pipelined_bytes, scratch_bytes=0, temp_bytes=0):
    need = 2 * pipelined_bytes + scratch_bytes + temp_bytes + (4 << 20)
    return int(min(max(need, 16 << 20), VMEM_CAP))


def _nbytes(shape, dtype):
    return int(np.prod(shape)) * jnp.dtype(dtype).itemsize


def _rms_scale(x):
    return x * lax.rsqrt(jnp.mean(x * x, axis=-1, keepdims=True) + RMS_EPS)


CAST_BLOCK_BYTES = 4 << 20


def _cast_kernel(w_ref, o_ref):
    o_ref[...] = w_ref[...].astype(o_ref.dtype)


def cast_bf16(w3, layer, *, name):
    _, k, n = w3.shape
    tk = k
    while tk % 2 == 0 and tk > 2 * SUBLANES and _nbytes((tk, n), F32) > CAST_BLOCK_BYTES:
        tk //= 2
    return pl.pallas_call(
        _cast_kernel,
        grid=(k // tk,),
        in_specs=[pl.BlockSpec((None, tk, n), lambda i: (layer, i, 0))],
        out_specs=pl.BlockSpec((tk, n), lambda i: (i, 0)),
        out_shape=jax.ShapeDtypeStruct((k, n), BF16),
        compiler_params=pltpu.CompilerParams(
            dimension_semantics=("parallel",),
            vmem_limit_bytes=_vmem_limit(_nbytes((tk, n), F32) + _nbytes((tk, n), BF16))),
        name=name,
    )(w3)


def _as_parts(v):
    return tuple(v) if isinstance(v, (tuple, list)) else (v,)


def _stacked_specs(parts, block, col_of):
    specs, starts = [], []
    off = 0
    for arr in parts:
        nb = arr.shape[0] // block[0]
        assert arr.shape[0] % block[0] == 0
        specs.append(pl.BlockSpec(block, lambda i, j, off=off, nb=nb: (jnp.clip(i - off, 0, nb - 1), col_of(j))))
        starts.append(off)
        off += nb
    return specs, tuple(starts), off


def _stacked_read(refs, starts):
    i = pl.program_id(0)
    v = refs[0][...]
    for r, s in zip(refs[1:], starts[1:]):
        v = jnp.where(i >= s, r[...], v)
    return v


def _norm_matmul_kernel(*refs, x_starts, has_bias, has_scale):
    refs = list(refs)
    x_refs = [refs.pop(0) for _ in x_starts]
    nw_ref, w_ref = refs.pop(0), refs.pop(0)
    b_ref = refs.pop(0) if has_bias else None
    s_ref = refs.pop(0) if has_scale else None
    o_ref, hn_ref = refs

    @pl.when(pl.program_id(1) == 0)
    def _():
        hn_ref[...] = (_rms_scale(_stacked_read(x_refs, x_starts)) * nw_ref[...]).astype(BF16)

    acc = jnp.dot(hn_ref[...], w_ref[...], preferred_element_type=F32)
    if has_bias:
        acc = acc + b_ref[...]
    if has_scale:
        acc = acc * s_ref[...]
    o_ref[...] = acc.astype(o_ref.dtype)


def norm_matmul(x, nw, w, out_dtype, *, b=None, scale=None, col0=0, n=None, tm, tn, name):
    xs = _as_parts(x)
    d = xs[0].shape[1]
    n = w.shape[1] if n is None else n
    assert n % tn == 0 and col0 % tn == 0
    jb0 = col0 // tn
    x_specs, x_starts, n_i = _stacked_specs(xs, (tm, d), lambda j: 0)
    blocks = (len(xs) * _nbytes((tm, d), F32) + _nbytes((d, tn), BF16) + _nbytes((tm, tn), out_dtype))
    vec_spec = pl.BlockSpec((1, tn), lambda i, j: (0, j))
    extra = [v.reshape(1, n) for v in (b, scale) if v is not None]
    return pl.pallas_call(
        functools.partial(_norm_matmul_kernel, x_starts=x_starts, has_bias=b is not None,
                          has_scale=scale is not None),
        grid=(n_i, n // tn),
        in_specs=x_specs + [
            pl.BlockSpec((1, d), lambda i, j: (0, 0)),
            pl.BlockSpec((d, tn), lambda i, j: (0, jb0 + j)),
        ] + [vec_spec] * len(extra),
        out_specs=pl.BlockSpec((tm, tn), lambda i, j: (i, j)),
        out_shape=jax.ShapeDtypeStruct((n_i * tm, n), out_dtype),
        scratch_shapes=[pltpu.VMEM((tm, d), BF16)],
        compiler_params=pltpu.CompilerParams(
            dimension_semantics=("parallel", "arbitrary"),
            vmem_limit_bytes=_vmem_limit(blocks, _nbytes((tm, d), BF16), _nbytes((tm, tn), F32))),
        name=name,
    )(*xs, nw.reshape(1, d), w, *extra)


def _matmul_res_kernel(*refs, a_starts, r_starts, has_bias):
    refs = list(refs)
    a_refs = [refs.pop(0) for _ in a_starts]
    w_ref = refs.pop(0)
    r_refs = [refs.pop(0) for _ in r_starts]
    b_ref = refs.pop(0) if has_bias else None
    (o_ref,) = refs
    acc = jnp.dot(_stacked_read(a_refs, a_starts), w_ref[...], preferred_element_type=F32)
    if has_bias:
        acc = acc + b_ref[...]
    o_ref[...] = _stacked_read(r_refs, r_starts) + acc


def matmul_res(a, w, res, *, b=None, tm, tn, name):
    a_parts, r_parts = _as_parts(a), _as_parts(res)
    k, n = w.shape
    assert n % tn == 0
    a_specs, a_starts, n_i = _stacked_specs(a_parts, (tm, k), lambda j: 0)
    r_specs, r_starts, n_r = _stacked_specs(r_parts, (tm, tn), lambda j: j)
    assert n_i == n_r
    blocks = (len(a_parts) * _nbytes((tm, k), BF16) + _nbytes((k, tn), BF16)
              + (len(r_parts) + 1) * _nbytes((tm, tn), F32))
    extra = [] if b is None else [b.reshape(1, n)]
    return pl.pallas_call(
        functools.partial(_matmul_res_kernel, a_starts=a_starts, r_starts=r_starts, has_bias=b is not None),
        grid=(n_i, n // tn),
        in_specs=a_specs + [pl.BlockSpec((k, tn), lambda i, j: (0, j))] + r_specs
        + [pl.BlockSpec((1, tn), lambda i, j: (0, j))] * len(extra),
        out_specs=pl.BlockSpec((tm, tn), lambda i, j: (i, j)),
        out_shape=jax.ShapeDtypeStruct((n_i * tm, n), F32),
        compiler_params=pltpu.CompilerParams(
            dimension_semantics=("parallel", "arbitrary"),
            vmem_limit_bytes=_vmem_limit(blocks, 0, _nbytes((tm, tn), F32))),
        name=name,
    )(*a_parts, w, *r_parts, *extra)


def _mlp_kernel(x_ref, nw_ref, wu_ref, wd_ref, fw_ref, o_ref, hn_ref, *, final):
    k = pl.program_id(1)

    @pl.when(k == 0)
    def _():
        x = x_ref[...]
        hn_ref[...] = (_rms_scale(x) * nw_ref[...]).astype(BF16)
        o_ref[...] = x

    h = jnp.dot(hn_ref[...], wu_ref[...], preferred_element_type=F32)
    h = jnp.square(jnp.maximum(h, 0.0)).astype(BF16)
    o_ref[...] += jnp.dot(h, wd_ref[...], preferred_element_type=F32)

    if final:
        @pl.when(k == pl.num_programs(1) - 1)
        def _():
            o_ref[...] = _rms_scale(o_ref[...]) * fw_ref[...]


def mlp(x, nw, wu, wd, fw, *, final, tm, th, name):
    t, d = x.shape
    hid = wu.shape[1]
    assert t % tm == 0 and hid % th == 0
    blocks = (2 * _nbytes((tm, d), F32) + _nbytes((d, th), BF16) + _nbytes((th, d), BF16))
    return pl.pallas_call(
        functools.partial(_mlp_kernel, final=final),
        grid=(t // tm, hid // th),
        in_specs=[
            pl.BlockSpec((tm, d), lambda i, k: (i, 0)),
            pl.BlockSpec((1, d), lambda i, k: (0, 0)),
            pl.BlockSpec((d, th), lambda i, k: (0, k)),
            pl.BlockSpec((th, d), lambda i, k: (k, 0)),
            pl.BlockSpec((1, d), lambda i, k: (0, 0)),
        ],
        out_specs=pl.BlockSpec((tm, d), lambda i, k: (i, 0)),
        out_shape=jax.ShapeDtypeStruct((t, d), F32),
        scratch_shapes=[pltpu.VMEM((tm, d), BF16)],
        compiler_params=pltpu.CompilerParams(
            dimension_semantics=("parallel", "arbitrary"),
            vmem_limit_bytes=_vmem_limit(blocks, _nbytes((tm, d), BF16),
                                         _nbytes((tm, th), F32) + _nbytes((tm, d), F32))),
        name=name,
    )(x, nw.reshape(1, d), wu, wd, fw.reshape(1, d))


LOG2_E = 1.4426950408889634
NA_WIN_TOK = NA_KH * GRID_W
NA_N_DR = 2 * NA_KH - 1
NA_ROWS_PER_ITER = 8


def _na_kernel(q_ref, k_ref, v_ref, rp_ref, o_ref, u_ref, s_ref, p_ref, l_ref, *, rows):
    W = GRID_W
    lane = lax.broadcasted_iota(jnp.int32, (W, LANES), 1)
    first_head = lane < NA_HEAD_DIM
    even_row = lane < W

    c = lax.broadcasted_iota(jnp.int32, (W, LANES), 0)
    kc = jnp.where(even_row, lane, lane - W)
    cs = jnp.clip(c - NA_KW // 2, 0, W - NA_KW)
    valid = (kc >= cs) & (kc < cs + NA_KW)
    shift = LANES - (NA_KW - 1)
    for h in range(2):
        t0_prev = None
        for dr in range(NA_N_DR):
            base = jnp.broadcast_to(rp_ref[h, dr:dr + 1, :] * LOG2_E, (W, LANES))
            t0 = pltpu.roll(base, shift, 1, stride=1, stride_axis=0)
            if dr >= 1:
                t1 = pltpu.roll(base, (shift + W) % LANES, 1, stride=1, stride_axis=0)
                u_ref[h, dr - 1] = jnp.where(valid, jnp.where(even_row, t0_prev, t1), -jnp.inf)
            t0_prev = t0

    RB = NA_ROWS_PER_ITER
    n_groups = rows // RB

    def win_start(r):
        return jnp.clip(r - NA_KH // 2, 0, rows - NA_KH)

    def scores(g):
        for u in range(RB):
            r = g * RB + u
            q0 = pl.multiple_of(r * W, W)
            k0 = pl.multiple_of(win_start(r) * W, W)
            q = q_ref[pl.ds(q0, W), :]
            zero = jnp.zeros_like(q)
            qs = jnp.concatenate([jnp.where(first_head, q, zero), jnp.where(first_head, zero, q)], axis=0)
            kk = k_ref[pl.ds(k0, NA_WIN_TOK), :]
            s_ref[u] = lax.dot_general(qs, kk, (((1,), (1,)), ((), ())), preferred_element_type=F32)

    def softmax(g):
        for u in range(RB):
            r = g * RB + u
            d0 = (NA_KH - 1) - (r - win_start(r))
            bias = jnp.concatenate(
                [jnp.concatenate([u_ref[h, d0 + 2 * j] for j in range(NA_KH // 2)], axis=1) for h in range(2)],
                axis=0)
            s = s_ref[u] + bias
            m = jnp.max(s, axis=-1, keepdims=True)
            p = jnp.exp2(s - m)
            p_ref[u] = p.astype(BF16)
            l_ref[u] = jnp.broadcast_to(jnp.sum(p, axis=-1, keepdims=True), (2 * W, LANES))

    def values(g):
        for u in range(RB):
            r = g * RB + u
            q0 = pl.multiple_of(r * W, W)
            k0 = pl.multiple_of(win_start(r) * W, W)
            vv = v_ref[pl.ds(k0, NA_WIN_TOK), :]
            pv = jnp.dot(p_ref[u], vv, preferred_element_type=F32) / l_ref[u]
            out = jnp.where(first_head, pv[:W], pv[W:])
            o_ref[pl.ds(q0, W), :] = out.astype(o_ref.dtype)

    scores(0)
    softmax(0)
    scores(1)

    def body(g, carry):
        values(g)
        softmax(g + 1)
        scores(g + 2)
        return carry

    lax.fori_loop(0, n_groups - 2, body, 0)
    values(n_groups - 2)
    softmax(n_groups - 1)
    values(n_groups - 1)


def na_core(qkv, rp, *, seq_len, seq_block0, n_seq, name):
    d = qkv.shape[1] // 3
    n_pairs = d // LANES
    rows = seq_len // GRID_W
    assert rows % NA_ROWS_PER_ITER == 0 and rows // NA_ROWS_PER_ITER >= 2 and rows >= NA_KH
    blk = (seq_len, LANES)
    u_shape = (2, NA_N_DR - 1, GRID_W, LANES)
    s_shape = (NA_ROWS_PER_ITER, 2 * GRID_W, NA_WIN_TOK)
    l_shape = (NA_ROWS_PER_ITER, 2 * GRID_W, LANES)
    scratch_bytes = (_nbytes(u_shape, F32) + _nbytes(s_shape, F32) + _nbytes(s_shape, BF16)
                     + _nbytes(l_shape, F32))
    blocks = 4 * _nbytes(blk, BF16) + _nbytes((2,) + rp.shape[1:], F32)
    return pl.pallas_call(
        functools.partial(_na_kernel, rows=rows),
        grid=(n_pairs, n_seq),
        in_specs=[
            pl.BlockSpec(blk, lambda h, b: (seq_block0 + b, h)),
            pl.BlockSpec(blk, lambda h, b: (seq_block0 + b, n_pairs + h)),
            pl.BlockSpec(blk, lambda h, b: (seq_block0 + b, 2 * n_pairs + h)),
            pl.BlockSpec((2,) + rp.shape[1:], lambda h, b: (h, 0, 0)),
        ],
        out_specs=pl.BlockSpec(blk, lambda h, b: (b, h)),
        out_shape=jax.ShapeDtypeStruct((n_seq * seq_len, d), BF16),
        scratch_shapes=[pltpu.VMEM(u_shape, F32), pltpu.VMEM(s_shape, F32), pltpu.VMEM(s_shape, BF16),
                        pltpu.VMEM(l_shape, F32)],
        compiler_params=pltpu.CompilerParams(
            dimension_semantics=("parallel", "arbitrary"),
            vmem_limit_bytes=_vmem_limit(blocks, scratch_bytes, 8 << 20)),
        name=name,
    )(qkv, qkv, qkv, rp)


def _na_pad_rpb(rpb):
    h, n_dr, n_dc = rpb.shape
    return jnp.pad(rpb.astype(F32), ((0, 0), (0, 2 * NA_KH - n_dr), (0, LANES - n_dc)))


def _split3(x):
    hi = x.astype(BF16)
    r1 = x - hi.astype(F32)
    mid = r1.astype(BF16)
    lo = (r1 - mid.astype(F32)).astype(BF16)
    return hi, mid, lo


def _dt_kernel(raw_ref, bias_ref, alog_ref, a_ref, ea_ref, w_ref, atd_ref, eend_ref, *, n_chunks, n_heads):
    L = SSM_CHUNK
    x = raw_ref[...] + bias_ref[...]
    dt = jnp.maximum(x, 0.0) + jnp.log1p(jnp.exp(-jnp.abs(x)))
    log2_dt = jnp.log(dt) * LOG2_E
    ad = dt * (-jnp.exp(alog_ref[...]))
    row = lax.broadcasted_iota(jnp.int32, (L, L), 0)
    col = lax.broadcasted_iota(jnp.int32, (L, L), 1)
    tri_f = (col <= row).astype(BF16)
    tri_b = (col >= row).astype(BF16)
    fwd_lane = lax.broadcasted_iota(jnp.int32, (L, 2 * n_heads), 1) < n_heads
    for c in range(n_chunks):
        rows = slice(c * L, (c + 1) * L)
        parts = _split3(ad[rows])
        f = sum(jnp.dot(tri_f, p, preferred_element_type=F32) for p in parts)
        b = sum(jnp.dot(tri_b, p, preferred_element_type=F32) for p in parts)
        a = jnp.where(fwd_lane, f, b)
        a_end = jnp.where(fwd_lane[:1], f[L - 1:L], b[0:1])
        a2 = a * LOG2_E
        w = dt[rows] * jnp.exp(a_end - a)
        a_ref[rows, :] = a2
        for part, (ve, vw) in enumerate(zip(_split3(jnp.exp(a)), _split3(w))):
            ea_ref[part, rows, :] = ve
            w_ref[part, rows, :] = vw
        atd_ref[c] = (a2 - log2_dt[rows]).T
        eend_ref[c] = jnp.broadcast_to(jnp.exp(a_end), a.shape).T


def ssd_steps(raw, dt_bias, a_log, *, tb, name):
    t, w = raw.shape
    n_chunks = tb // SSM_CHUNK
    assert t % tb == 0 and w == LANES
    tok_spec = pl.BlockSpec((tb, w), lambda i: (i, 0))
    hm_spec = pl.BlockSpec((n_chunks, w, SSM_CHUNK), lambda i: (i, 0, 0))
    hm_shape = jax.ShapeDtypeStruct((t // SSM_CHUNK, w, SSM_CHUNK), F32)
    split_spec = pl.BlockSpec((3, tb, w), lambda i: (0, i, 0))
    split_shape = jax.ShapeDtypeStruct((3, t, w), BF16)
    blocks = 3 * _nbytes((tb, w), F32) + 3 * _nbytes((n_chunks, w, SSM_CHUNK), F32)
    return pl.pallas_call(
        functools.partial(_dt_kernel, n_chunks=n_chunks, n_heads=w // 2),
        grid=(t // tb,),
        in_specs=[
            tok_spec,
            pl.BlockSpec((1, w), lambda i: (0, 0)),
            pl.BlockSpec((1, w), lambda i: (0, 0)),
        ],
        out_specs=[tok_spec, split_spec, split_spec, hm_spec, hm_spec],
        out_shape=[jax.ShapeDtypeStruct((t, w), F32), split_shape, split_shape, hm_shape, hm_shape],
        compiler_params=pltpu.CompilerParams(
            dimension_semantics=("parallel",),
            vmem_limit_bytes=_vmem_limit(blocks, 0, 4 << 20)),
        name=name,
    )(raw, dt_bias.reshape(1, w), a_log.reshape(1, w))


HALO = SUBLANES
SSD_SCAN_TB = 512


def _any_equal(v, values):
    return functools.reduce(jnp.logical_or, [v == s for s in values])


def _conv_kernel(prev_ref, main_ref, next_ref, w_ref, b_ref, o_ref, *, tb, starts, ends):
    start = pl.program_id(0) * tb
    at_start = _any_equal(start, starts)
    at_end = _any_equal(start + tb, ends)
    buf = jnp.concatenate([jnp.where(at_start, 0.0, prev_ref[...]), main_ref[...],
                           jnp.where(at_end, 0.0, next_ref[...])], axis=0)
    n = tb + 2 * HALO
    acc = b_ref[...] + main_ref[...] * w_ref[SSM_CONV // 2:SSM_CONV // 2 + 1, :]
    for k in range(SSM_CONV):
        if k != SSM_CONV // 2:
            rolled = pltpu.roll(buf, (SSM_CONV // 2 - k) % n, 0)
            acc = acc + rolled[HALO:HALO + tb] * w_ref[k:k + 1, :]
    o_ref[...] = (acc / (1.0 + jnp.exp(-acc))).astype(o_ref.dtype)


def ssd_conv(zx, conv_w, conv_b, *, col0, tb, tc, starts, ends, name):
    t = zx.shape[0]
    c = conv_w.shape[1]
    assert t % tb == 0 and c % tc == 0 and col0 % tc == 0 and tb % HALO == 0
    cb0 = col0 // tc
    hb = tb // HALO
    n_hb = t // HALO
    blocks = (_nbytes((tb, tc), F32) + _nbytes((tb, tc), BF16))
    return pl.pallas_call(
        functools.partial(_conv_kernel, tb=tb, starts=starts, ends=ends),
        grid=(t // tb, c // tc),
        in_specs=[
            pl.BlockSpec((HALO, tc), lambda i, j: (jnp.maximum(i * hb - 1, 0), cb0 + j)),
            pl.BlockSpec((tb, tc), lambda i, j: (i, cb0 + j)),
            pl.BlockSpec((HALO, tc), lambda i, j: (jnp.minimum((i + 1) * hb, n_hb - 1), cb0 + j)),
            pl.BlockSpec((SSM_CONV, tc), lambda i, j: (0, j)),
            pl.BlockSpec((1, tc), lambda i, j: (0, j)),
        ],
        out_specs=pl.BlockSpec((tb, tc), lambda i, j: (i, j)),
        out_shape=jax.ShapeDtypeStruct((t, c), BF16),
        compiler_params=pltpu.CompilerParams(
            dimension_semantics=("parallel", "arbitrary"),
            vmem_limit_bytes=_vmem_limit(blocks, 0, 8 * _nbytes((tb, tc), F32))),
        name=name,
    )(zx, zx, zx, conv_w, conv_b.reshape(1, c))


def _scan_kernel(x_ref, b_ref, c_ref, a_ref, ea3_ref, w3_ref, atd_ref, eend_ref, *rest, tb, bounds, reverse):
    if reverse:
        yf_ref, z_ref, dsk_ref, nw_ref, o_ref, state_ref, y_ref = rest
    else:
        o_ref, state_ref = rest
        y_ref = o_ref
    L = SSM_CHUNK
    P = SSM_HEAD_DIM
    N = SSM_STATE
    n_chunks = tb // L
    heads = x_ref.shape[1] // P
    k3 = ea3_ref.shape[1]
    col_head = lax.broadcasted_iota(jnp.int32, (k3, heads * P), 0) % heads
    lane_head = lax.broadcasted_iota(jnp.int32, (k3, heads * P), 1) // P
    exp_head = jnp.where(col_head == lane_head, 1.0, 0.0).astype(BF16)
    i = pl.program_id(1)
    blk = (pl.num_programs(1) - 1 - i) if reverse else i
    edge = (blk + 1) * tb if reverse else blk * tb

    @pl.when(_any_equal(edge, bounds))
    def _():
        state_ref[...] = jnp.zeros_like(state_ref)

    row = lax.broadcasted_iota(jnp.int32, (L, L), 0)
    col = lax.broadcasted_iota(jnp.int32, (L, L), 1)
    tri = (row <= col) if reverse else (row >= col)
    first_head = lax.broadcasted_iota(jnp.int32, (L, LANES), 1) < P
    first_head_n = lax.broadcasted_iota(jnp.int32, (N, LANES), 1) < P

    def chunk(j, carry):
        c = (n_chunks - 1 - j) if reverse else j
        rows = pl.ds(pl.multiple_of(c * L, L), L)
        x = x_ref[rows, :]
        bm = b_ref[rows, :]
        cm = c_ref[rows, :]
        atd = atd_ref[c]
        eend = eend_ref[c]
        a = a_ref[rows, :]
        ea_x = jnp.dot(ea3_ref[rows, :], exp_head, preferred_element_type=F32)
        w_x = jnp.dot(w3_ref[rows, :], exp_head, preferred_element_type=F32)
        cb = lax.dot_general(cm, bm, (((1,), (1,)), ((), ())), preferred_element_type=F32)
        y_off = jnp.dot(cm, state_ref[...].astype(BF16), preferred_element_type=F32) * ea_x
        xw = (x.astype(F32) * w_x).astype(BF16)
        z_new = lax.dot_general(bm, xw, (((0,), (0,)), ((), ())), preferred_element_type=F32)
        for pair in range(heads // 2):
            cols = slice(pair * LANES, (pair + 1) * LANES)
            xp = x[:, cols]
            ys = []
            for h in (2 * pair, 2 * pair + 1):
                seg = a[:, h:h + 1] - atd[h:h + 1, :]
                m = cb * jnp.exp2(jnp.where(tri, seg, -jnp.inf))
                ys.append(jnp.dot(m.astype(BF16), xp, preferred_element_type=F32))
            e_pair = jnp.where(first_head_n[:1], eend[2 * pair:2 * pair + 1, :], eend[2 * pair + 1:2 * pair + 2, :])
            state_ref[:, cols] = state_ref[:, cols] * e_pair + z_new[:, cols]
            y_pair = jnp.where(first_head, ys[0], ys[1]) + y_off[:, cols]
            if reverse:
                y_ref[:, cols] = y_pair
            else:
                y_ref[rows, cols] = y_pair

        if reverse:
            y = y_ref[...] + yf_ref[rows, :] + x.astype(F32) * dsk_ref[...]
            zz = z_ref[rows, :]
            y = y * (zz / (1.0 + jnp.exp(-zz)))
            o_ref[rows, :] = (_rms_scale(y) * nw_ref[...]).astype(o_ref.dtype)
        return carry

    lax.fori_loop(0, n_chunks, chunk, 0, unroll=True)


def ssd_scan(xc, ag, ea3g, w3g, atd, eend, extra, *, d_inner, tb, bounds, reverse, name):
    t = xc.shape[0]
    G = SSM_GROUPS
    gw = d_inner // G
    hpg = gw // SSM_HEAD_DIM
    n_blk = t // tb
    n_chunks = tb // SSM_CHUNK
    assert t % tb == 0 and gw % LANES == 0
    xb0 = d_inner // SSM_STATE
    cb0 = xb0 + G
    at_row0 = (atd.shape[1] // 2 // hpg) if reverse else 0

    def tok(i):
        return (n_blk - 1 - i) if reverse else i

    direction = 1 if reverse else 0
    tok_spec = pl.BlockSpec((None, None, tb, hpg), lambda g, i: (direction, g, tok(i), 0))
    tok3_spec = pl.BlockSpec((None, None, tb, 3 * hpg), lambda g, i: (direction, g, tok(i), 0))
    hm_spec = pl.BlockSpec((n_chunks, hpg, SSM_CHUNK), lambda g, i: (tok(i), at_row0 + g, 0))
    in_specs = [
        pl.BlockSpec((tb, gw), lambda g, i: (tok(i), g)),
        pl.BlockSpec((tb, SSM_STATE), lambda g, i: (tok(i), xb0 + g)),
        pl.BlockSpec((tb, SSM_STATE), lambda g, i: (tok(i), cb0 + g)),
        tok_spec, tok3_spec, tok3_spec, hm_spec, hm_spec,
    ]
    args = [xc, xc, xc, ag, ea3g, w3g, atd, eend]
    blocks = (_nbytes((tb, gw), BF16) + 2 * _nbytes((tb, SSM_STATE), BF16)
              + 3 * _nbytes((tb, LANES), BF16) + 2 * _nbytes((n_chunks, hpg, SSM_CHUNK), F32)
              + _nbytes((tb, gw), F32))
    if reverse:
        yf, zx, dsk, nw = extra
        in_specs += [
            pl.BlockSpec((tb, gw), lambda g, i: (tok(i), g)),
            pl.BlockSpec((tb, gw), lambda g, i: (tok(i), g)),
            pl.BlockSpec((1, gw), lambda g, i: (0, g)),
            pl.BlockSpec((1, gw), lambda g, i: (0, g)),
        ]
        args += [yf, zx, dsk, nw]
        blocks += 2 * _nbytes((tb, gw), F32)
    scratch = [pltpu.VMEM((SSM_STATE, gw), F32)]
    if reverse:
        scratch.append(pltpu.VMEM((SSM_CHUNK, gw), F32))
    return pl.pallas_call(
        functools.partial(_scan_kernel, tb=tb, bounds=bounds, reverse=reverse),
        grid=(G, n_blk),
        in_specs=in_specs,
        out_specs=pl.BlockSpec((tb, gw), lambda g, i: (tok(i), g)),
        out_shape=jax.ShapeDtypeStruct((t, d_inner), BF16 if reverse else F32),
        scratch_shapes=scratch,
        compiler_params=pltpu.CompilerParams(
            dimension_semantics=("parallel", "arbitrary"),
            vmem_limit_bytes=_vmem_limit(blocks, _nbytes((SSM_STATE, gw), F32), 8 << 20)),
        name=name,
    )(*args)


def _mixer_na(x, nw, qkv_w, qkv_b, rpb, out_w, out_b, seqs, li):
    d = _as_parts(x)[0].shape[1]
    tm = 1024 // len(_as_parts(x))
    q_scale = jnp.concatenate([jnp.full((d,), NA_HEAD_DIM ** -0.5 * LOG2_E, F32), jnp.ones((2 * d,), F32)])
    qkv = norm_matmul(x, nw, qkv_w, BF16, b=qkv_b, scale=q_scale, tm=tm, tn=1024, name=f"na_qkv_{li}")
    rp = _na_pad_rpb(rpb)
    outs = []
    tok0 = 0
    for si, (seq_len, n_seq) in enumerate(seqs):
        assert tok0 % seq_len == 0
        outs.append(na_core(qkv, rp, seq_len=seq_len, seq_block0=tok0 // seq_len,
                            n_seq=n_seq, name=f"na_core_{li}_{si}"))
        tok0 += seq_len * n_seq
    return matmul_res(tuple(outs), out_w, x, b=out_b, tm=1024, tn=1024, name=f"na_out_{li}")


def _mixer_ssd(x, nw, in_w, conv_w, conv_b, dt_bias, a_log, d_skip, norm_w, out_w, starts, ends, li):
    d_inner = out_w.shape[0]
    conv_dim = conv_w.shape[1]
    n_main = d_inner + conv_dim
    n_dt = in_w.shape[1] - n_main
    heads = d_inner // SSM_HEAD_DIM
    hpg = heads // SSM_GROUPS
    t = x.shape[0]
    zx = norm_matmul(x, nw, in_w, F32, n=n_main, tm=1024, tn=1024, name=f"ssd_in_{li}")
    raw = norm_matmul(x, nw, in_w, F32, col0=n_main, n=n_dt, tm=1024, tn=n_dt, name=f"ssd_dt_{li}")
    a, ea3, w3, atd, eend = ssd_steps(raw, dt_bias, a_log, tb=1024, name=f"ssd_steps_{li}")
    xc = ssd_conv(zx, conv_w, conv_b, col0=d_inner, tb=512, tc=512, starts=starts, ends=ends, name=f"ssd_conv_{li}")

    def per_group(v):
        parts = v.shape[0]
        g = jnp.transpose(v.reshape(parts, t, 2, SSM_GROUPS, hpg), (2, 3, 1, 0, 4))
        return g.reshape(2, SSM_GROUPS, t, parts * hpg)

    ag = per_group(a[None])
    ea3g = per_group(ea3)
    w3g = per_group(w3)
    yf = ssd_scan(xc, ag, ea3g, w3g, atd, eend, None, d_inner=d_inner, tb=SSD_SCAN_TB,
                  bounds=starts, reverse=False, name=f"ssd_fwd_{li}")
    dsk = jnp.repeat(d_skip.astype(F32), SSM_HEAD_DIM).reshape(1, d_inner)
    y = ssd_scan(xc, ag, ea3g, w3g, atd, eend, (yf, zx, dsk, norm_w.reshape(1, d_inner)),
                 d_inner=d_inner, tb=SSD_SCAN_TB, bounds=ends, reverse=True, name=f"ssd_bwd_{li}")
    return matmul_res(y, out_w, x, tm=1024, tn=1024, name=f"ssd_out_{li}")


def kernel(x_prompt, x_sample, mix_norm, na_qkv_w, na_qkv_b, na_rpb, na_out_w, na_out_b, ssm_in_w, ssm_conv_w, ssm_conv_b, ssm_dt_bias, ssm_a_log, ssm_d, ssm_norm_w, ssm_out_w, mlp_norm, mlp_up_w, mlp_down_w, final_norm):
    bp, tp, d = x_prompt.shape
    bs, ts, _ = x_sample.shape
    depth = mix_norm.shape[0]
    x = jnp.concatenate([x_prompt.reshape(bp * tp, d), x_sample.reshape(bs * ts, d)], axis=0)
    seqs = ((tp, bp), (ts, bs))
    starts, ends = [], []
    tok = 0
    for seq_len, n_seq in seqs:
        for _ in range(n_seq):
            starts.append(tok)
            tok += seq_len
            ends.append(tok)
    starts, ends = tuple(starts), tuple(ends)

    for i in range(depth):
        j = i // N_MIXERS
        if i % N_MIXERS == 0:
            x = _mixer_na(x, mix_norm[i], cast_bf16(na_qkv_w, j, name=f"cast_qkv_{i}"), na_qkv_b[j], na_rpb[j],
                          cast_bf16(na_out_w, j, name=f"cast_na_out_{i}"), na_out_b[j], seqs, i)
        else:
            x = _mixer_ssd(x, mix_norm[i], cast_bf16(ssm_in_w, j, name=f"cast_ssd_in_{i}"), ssm_conv_w[j],
                           ssm_conv_b[j], ssm_dt_bias[j], ssm_a_log[j], ssm_d[j], ssm_norm_w[j],
                           cast_bf16(ssm_out_w, j, name=f"cast_ssd_out_{i}"), starts, ends, i)
        x = mlp(x, mlp_norm[i], cast_bf16(mlp_up_w, i, name=f"cast_up_{i}"),
                cast_bf16(mlp_down_w, i, name=f"cast_down_{i}"), final_norm,
                final=(i == depth - 1), tm=1024, th=512, name=f"mlp_{i}")
    y_prompt = x[:bp * tp].reshape(bp, tp, d)
    y_sample = x[bp * tp:].reshape(bs, ts, d)
    return (y_prompt, y_sample)
```

```python
import functools

import numpy as np
import jax
import jax.numpy as jnp
from jax import lax
from jax.experimental import pallas as pl
from jax.experimental.pallas import tpu as pltpu

F32 = jnp.float32
BF16 = jnp.bfloat16

GRID_W = 64
NA_HEADS = 32
NA_HEAD_DIM = 64
NA_KH = 8
NA_KW = 16
SSM_HEAD_DIM = 64
SSM_GROUPS = 8
SSM_STATE = 128
SSM_CONV = 5
SSM_CHUNK = 128
N_MIXERS = 2
RMS_EPS = 1e-5

LANES = 128
SUBLANES = 8
VMEM_BYTES = 64 * 1024 * 1024
VMEM_CAP = VMEM_BYTES - 8 * 1024 * 1024


def _vmem_limit(pipelined_bytes, scratch_bytes=0, temp_bytes=0):
    need = 2 * pipelined_bytes + scratch_bytes + temp_bytes + (4 << 20)
    return int(min(max(need, 16 << 20), VMEM_CAP))


def _nbytes(shape, dtype):
    return int(np.prod(shape)) * jnp.dtype(dtype).itemsize


def _rms_scale(x):
    return x * lax.rsqrt(jnp.mean(x * x, axis=-1, keepdims=True) + RMS_EPS)


CAST_BLOCK_BYTES = 4 << 20


def _cast_kernel(w_ref, o_ref):
    o_ref[...] = w_ref[...].astype(o_ref.dtype)


def cast_bf16(w3, layer, *, name):
    _, k, n = w3.shape
    tk = k
    while tk % 2 == 0 and tk > 2 * SUBLANES and _nbytes((tk, n), F32) > CAST_BLOCK_BYTES:
        tk //= 2
    return pl.pallas_call(
        _cast_kernel,
        grid=(k // tk,),
        in_specs=[pl.BlockSpec((None, tk, n), lambda i: (layer, i, 0))],
        out_specs=pl.BlockSpec((tk, n), lambda i: (i, 0)),
        out_shape=jax.ShapeDtypeStruct((k, n), BF16),
        compiler_params=pltpu.CompilerParams(
            dimension_semantics=("parallel",),
            vmem_limit_bytes=_vmem_limit(_nbytes((tk, n), F32) + _nbytes((tk, n), BF16))),
        name=name,
    )(w3)


def _as_parts(v):
    return tuple(v) if isinstance(v, (tuple, list)) else (v,)


def _stacked_specs(parts, block, col_of):
    specs, starts = [], []
    off = 0
    for arr in parts:
        nb = arr.shape[0] // block[0]
        assert arr.shape[0] % block[0] == 0
        specs.append(pl.BlockSpec(block, lambda i, j, off=off, nb=nb: (jnp.clip(i - off, 0, nb - 1), col_of(j))))
        starts.append(off)
        off += nb
    return specs, tuple(starts), off


def _stacked_read(refs, starts):
    i = pl.program_id(0)
    v = refs[0][...]
    for r, s in zip(refs[1:], starts[1:]):
        v = jnp.where(i >= s, r[...], v)
    return v


def _norm_matmul_kernel(x_ref, nw_ref, w_ref, *refs, has_bias, has_scale, has_side):
    refs = list(refs)
    ws_ref = refs.pop(0) if has_side else None
    b_ref = refs.pop(0) if has_bias else None
    s_ref = refs.pop(0) if has_scale else None
    o_ref = refs.pop(0)
    os_ref = refs.pop(0) if has_side else None
    (hn_ref,) = refs

    @pl.when(pl.program_id(1) == 0)
    def _():
        hn = (_rms_scale(x_ref[...]) * nw_ref[...]).astype(BF16)
        hn_ref[...] = hn
        if has_side:
            os_ref[...] = jnp.dot(hn, ws_ref[...], preferred_element_type=F32)

    acc = jnp.dot(hn_ref[...], w_ref[...], preferred_element_type=F32)
    if has_bias:
        acc = acc + b_ref[...]
    if has_scale:
        acc = acc * s_ref[...]
    o_ref[...] = acc.astype(o_ref.dtype)


def norm_matmul(x, nw, w, out_dtype, *, b=None, scale=None, n=None, side=None, tm, tn, name):
    t, d = x.shape
    n = w.shape[1] if n is None else n
    assert t % tm == 0 and n % tn == 0
    blocks = (_nbytes((tm, d), F32) + _nbytes((d, tn), BF16) + _nbytes((tm, tn), out_dtype))
    vec_spec = pl.BlockSpec((1, tn), lambda i, j: (0, j))
    in_specs = [
        pl.BlockSpec((tm, d), lambda i, j: (i, 0)),
        pl.BlockSpec((1, d), lambda i, j: (0, 0)),
        pl.BlockSpec((d, tn), lambda i, j: (0, j)),
    ]
    args = [x, nw.reshape(1, d), w]
    out_specs = [pl.BlockSpec((tm, tn), lambda i, j: (i, j))]
    out_shape = [jax.ShapeDtypeStruct((t, n), out_dtype)]
    if side is not None:
        col0, ns = side
        assert col0 % ns == 0
        in_specs.append(pl.BlockSpec((d, ns), lambda i, j: (0, col0 // ns)))
        args.append(w)
        out_specs.append(pl.BlockSpec((tm, ns), lambda i, j: (i, 0)))
        out_shape.append(jax.ShapeDtypeStruct((t, ns), F32))
        blocks += _nbytes((d, ns), BF16) + _nbytes((tm, ns), F32)
    for v in (b, scale):
        if v is not None:
            in_specs.append(vec_spec)
            args.append(v.reshape(1, n))
    outs = pl.pallas_call(
        functools.partial(_norm_matmul_kernel, has_bias=b is not None, has_scale=scale is not None,
                          has_side=side is not None),
        grid=(t // tm, n // tn),
        in_specs=in_specs,
        out_specs=out_specs,
        out_shape=out_shape,
        scratch_shapes=[pltpu.VMEM((tm, d), BF16)],
        compiler_params=pltpu.CompilerParams(
            dimension_semantics=("parallel", "arbitrary"),
            vmem_limit_bytes=_vmem_limit(blocks, _nbytes((tm, d), BF16), _nbytes((tm, tn), F32))),
        name=name,
    )(*args)
    return outs if side is not None else outs[0]


def _matmul_res_kernel(*refs, a_starts, r_starts, has_bias):
    refs = list(refs)
    a_refs = [refs.pop(0) for _ in a_starts]
    w_ref = refs.pop(0)
    r_refs = [refs.pop(0) for _ in r_starts]
    b_ref = refs.pop(0) if has_bias else None
    (o_ref,) = refs
    acc = jnp.dot(_stacked_read(a_refs, a_starts), w_ref[...], preferred_element_type=F32)
    if has_bias:
        acc = acc + b_ref[...]
    o_ref[...] = _stacked_read(r_refs, r_starts) + acc


def matmul_res(a, w, res, *, b=None, tm, tn, name):
    a_parts, r_parts = _as_parts(a), _as_parts(res)
    k, n = w.shape
    assert n % tn == 0
    a_specs, a_starts, n_i = _stacked_specs(a_parts, (tm, k), lambda j: 0)
    r_specs, r_starts, n_r = _stacked_specs(r_parts, (tm, tn), lambda j: j)
    assert n_i == n_r
    blocks = (len(a_parts) * _nbytes((tm, k), BF16) + _nbytes((k, tn), BF16)
              + (len(r_parts) + 1) * _nbytes((tm, tn), F32))
    extra = [] if b is None else [b.reshape(1, n)]
    return pl.pallas_call(
        functools.partial(_matmul_res_kernel, a_starts=a_starts, r_starts=r_starts, has_bias=b is not None),
        grid=(n_i, n // tn),
        in_specs=a_specs + [pl.BlockSpec((k, tn), lambda i, j: (0, j))] + r_specs
        + [pl.BlockSpec((1, tn), lambda i, j: (0, j))] * len(extra),
        out_specs=pl.BlockSpec((tm, tn), lambda i, j: (i, j)),
        out_shape=jax.ShapeDtypeStruct((n_i * tm, n), F32),
        compiler_params=pltpu.CompilerParams(
            dimension_semantics=("parallel", "arbitrary"),
            vmem_limit_bytes=_vmem_limit(blocks, 0, _nbytes((tm, tn), F32))),
        name=name,
    )(*a_parts, w, *r_parts, *extra)


def _mlp_kernel(x_ref, nw_ref, wu_ref, wd_ref, fw_ref, o_ref, hn_ref, *, final):
    k = pl.program_id(1)

    @pl.when(k == 0)
    def _():
        x = x_ref[...]
        hn_ref[...] = (_rms_scale(x) * nw_ref[...]).astype(BF16)
        o_ref[...] = x

    h = jnp.dot(hn_ref[...], wu_ref[...], preferred_element_type=F32)
    h = jnp.square(jnp.maximum(h, 0.0)).astype(BF16)
    o_ref[...] += jnp.dot(h, wd_ref[...], preferred_element_type=F32)

    if final:
        @pl.when(k == pl.num_programs(1) - 1)
        def _():
            o_ref[...] = _rms_scale(o_ref[...]) * fw_ref[...]


def mlp(x, nw, wu, wd, fw, *, final, tm, th, name):
    t, d = x.shape
    hid = wu.shape[1]
    assert t % tm == 0 and hid % th == 0
    blocks = (2 * _nbytes((tm, d), F32) + _nbytes((d, th), BF16) + _nbytes((th, d), BF16))
    return pl.pallas_call(
        functools.partial(_mlp_kernel, final=final),
        grid=(t // tm, hid // th),
        in_specs=[
            pl.BlockSpec((tm, d), lambda i, k: (i, 0)),
            pl.BlockSpec((1, d), lambda i, k: (0, 0)),
            pl.BlockSpec((d, th), lambda i, k: (0, k)),
            pl.BlockSpec((th, d), lambda i, k: (k, 0)),
            pl.BlockSpec((1, d), lambda i, k: (0, 0)),
        ],
        out_specs=pl.BlockSpec((tm, d), lambda i, k: (i, 0)),
        out_shape=jax.ShapeDtypeStruct((t, d), F32),
        scratch_shapes=[pltpu.VMEM((tm, d), BF16)],
        compiler_params=pltpu.CompilerParams(
            dimension_semantics=("parallel", "arbitrary"),
            vmem_limit_bytes=_vmem_limit(blocks, _nbytes((tm, d), BF16),
                                         _nbytes((tm, th), F32) + _nbytes((tm, d), F32))),
        name=name,
    )(x, nw.reshape(1, d), wu, wd, fw.reshape(1, d))


LOG2_E = 1.4426950408889634
NA_WIN_TOK = NA_KH * GRID_W
NA_N_DR = 2 * NA_KH - 1
NA_ROWS_PER_ITER = 8


def _na_kernel(q_ref, k_ref, v_ref, rp_ref, o_ref, u_ref, s_ref, p_ref, l_ref, *, rows):
    W = GRID_W
    lane = lax.broadcasted_iota(jnp.int32, (W, LANES), 1)
    first_head = lane < NA_HEAD_DIM
    even_row = lane < W

    c = lax.broadcasted_iota(jnp.int32, (W, LANES), 0)
    kc = jnp.where(even_row, lane, lane - W)
    cs = jnp.clip(c - NA_KW // 2, 0, W - NA_KW)
    valid = (kc >= cs) & (kc < cs + NA_KW)
    shift = LANES - (NA_KW - 1)
    for h in range(2):
        t0_prev = None
        for dr in range(NA_N_DR):
            base = jnp.broadcast_to(rp_ref[h, dr:dr + 1, :] * LOG2_E, (W, LANES))
            t0 = pltpu.roll(base, shift, 1, stride=1, stride_axis=0)
            if dr >= 1:
                t1 = pltpu.roll(base, (shift + W) % LANES, 1, stride=1, stride_axis=0)
                u_ref[h, dr - 1] = jnp.where(valid, jnp.where(even_row, t0_prev, t1), -jnp.inf)
            t0_prev = t0

    RB = NA_ROWS_PER_ITER
    n_groups = rows // RB

    def win_start(r):
        return jnp.clip(r - NA_KH // 2, 0, rows - NA_KH)

    def scores(g):
        for u in range(RB):
            r = g * RB + u
            q0 = pl.multiple_of(r * W, W)
            k0 = pl.multiple_of(win_start(r) * W, W)
            q = q_ref[pl.ds(q0, W), :]
            zero = jnp.zeros_like(q)
            qs = jnp.concatenate([jnp.where(first_head, q, zero), jnp.where(first_head, zero, q)], axis=0)
            kk = k_ref[pl.ds(k0, NA_WIN_TOK), :]
            s_ref[u] = lax.dot_general(qs, kk, (((1,), (1,)), ((), ())), preferred_element_type=F32)

    def softmax(g):
        for u in range(RB):
            r = g * RB + u
            d0 = (NA_KH - 1) - (r - win_start(r))
            bias = jnp.concatenate(
                [jnp.concatenate([u_ref[h, d0 + 2 * j] for j in range(NA_KH // 2)], axis=1) for h in range(2)],
                axis=0)
            s = s_ref[u] + bias
            m = jnp.max(s, axis=-1, keepdims=True)
            p = jnp.exp2(s - m)
            p_ref[u] = p.astype(BF16)
            l_ref[u] = jnp.broadcast_to(jnp.sum(p, axis=-1, keepdims=True), (2 * W, LANES))

    def values(g):
        for u in range(RB):
            r = g * RB + u
            q0 = pl.multiple_of(r * W, W)
            k0 = pl.multiple_of(win_start(r) * W, W)
            vv = v_ref[pl.ds(k0, NA_WIN_TOK), :]
            pv = jnp.dot(p_ref[u], vv, preferred_element_type=F32) / l_ref[u]
            out = jnp.where(first_head, pv[:W], pv[W:])
            o_ref[pl.ds(q0, W), :] = out.astype(o_ref.dtype)

    scores(0)
    softmax(0)
    scores(1)

    def body(g, carry):
        values(g)
        softmax(g + 1)
        scores(g + 2)
        return carry

    lax.fori_loop(0, n_groups - 2, body, 0)
    values(n_groups - 2)
    softmax(n_groups - 1)
    values(n_groups - 1)


def na_core(qkv, rp, *, seq_len, seq_block0, n_seq, name):
    d = qkv.shape[1] // 3
    n_pairs = d // LANES
    rows = seq_len // GRID_W
    assert rows % NA_ROWS_PER_ITER == 0 and rows // NA_ROWS_PER_ITER >= 2 and rows >= NA_KH
    blk = (seq_len, LANES)
    u_shape = (2, NA_N_DR - 1, GRID_W, LANES)
    s_shape = (NA_ROWS_PER_ITER, 2 * GRID_W, NA_WIN_TOK)
    l_shape = (NA_ROWS_PER_ITER, 2 * GRID_W, LANES)
    scratch_bytes = (_nbytes(u_shape, F32) + _nbytes(s_shape, F32) + _nbytes(s_shape, BF16)
                     + _nbytes(l_shape, F32))
    blocks = 4 * _nbytes(blk, BF16) + _nbytes((2,) + rp.shape[1:], F32)
    return pl.pallas_call(
        functools.partial(_na_kernel, rows=rows),
        grid=(n_pairs, n_seq),
        in_specs=[
            pl.BlockSpec(blk, lambda h, b: (seq_block0 + b, h)),
            pl.BlockSpec(blk, lambda h, b: (seq_block0 + b, n_pairs + h)),
            pl.BlockSpec(blk, lambda h, b: (seq_block0 + b, 2 * n_pairs + h)),
            pl.BlockSpec((2,) + rp.shape[1:], lambda h, b: (h, 0, 0)),
        ],
        out_specs=pl.BlockSpec(blk, lambda h, b: (b, h)),
        out_shape=jax.ShapeDtypeStruct((n_seq * seq_len, d), BF16),
        scratch_shapes=[pltpu.VMEM(u_shape, F32), pltpu.VMEM(s_shape, F32), pltpu.VMEM(s_shape, BF16),
                        pltpu.VMEM(l_shape, F32)],
        compiler_params=pltpu.CompilerParams(
            dimension_semantics=("parallel", "arbitrary"),
            vmem_limit_bytes=_vmem_limit(blocks, scratch_bytes, 8 << 20)),
        name=name,
    )(qkv, qkv, qkv, rp)


def _na_pad_rpb(rpb):
    h, n_dr, n_dc = rpb.shape
    return jnp.pad(rpb.astype(F32), ((0, 0), (0, 2 * NA_KH - n_dr), (0, LANES - n_dc)))


def _split3(x):
    hi = x.astype(BF16)
    r1 = x - hi.astype(F32)
    mid = r1.astype(BF16)
    lo = (r1 - mid.astype(F32)).astype(BF16)
    return hi, mid, lo


def _dt_kernel(raw_ref, bias_ref, alog_ref, a_ref, ea_ref, w_ref, atd_ref, eend_ref, *, n_chunks, n_heads):
    L = SSM_CHUNK
    x = raw_ref[...] + bias_ref[...]
    dt = jnp.maximum(x, 0.0) + jnp.log1p(jnp.exp(-jnp.abs(x)))
    log2_dt = jnp.log(dt) * LOG2_E
    ad = dt * (-jnp.exp(alog_ref[...]))
    row = lax.broadcasted_iota(jnp.int32, (L, L), 0)
    col = lax.broadcasted_iota(jnp.int32, (L, L), 1)
    tri_f = (col <= row).astype(BF16)
    tri_b = (col >= row).astype(BF16)
    fwd_lane = lax.broadcasted_iota(jnp.int32, (L, 2 * n_heads), 1) < n_heads
    for c in range(n_chunks):
        rows = slice(c * L, (c + 1) * L)
        parts = _split3(ad[rows])
        f = sum(jnp.dot(tri_f, p, preferred_element_type=F32) for p in parts)
        b = sum(jnp.dot(tri_b, p, preferred_element_type=F32) for p in parts)
        a = jnp.where(fwd_lane, f, b)
        a_end = jnp.where(fwd_lane[:1], f[L - 1:L], b[0:1])
        a2 = a * LOG2_E
        w = dt[rows] * jnp.exp(a_end - a)
        a_ref[rows, :] = a2
        for part, (ve, vw) in enumerate(zip(_split3(jnp.exp(a)), _split3(w))):
            ea_ref[part, rows, :] = ve
            w_ref[part, rows, :] = vw
        atd_ref[c] = (a2 - log2_dt[rows]).T
        eend_ref[c] = jnp.broadcast_to(jnp.exp(a_end), a.shape).T


def ssd_steps(raw, dt_bias, a_log, *, tb, name):
    t, w = raw.shape
    n_chunks = tb // SSM_CHUNK
    assert t % tb == 0 and w == LANES
    tok_spec = pl.BlockSpec((tb, w), lambda i: (i, 0))
    hm_spec = pl.BlockSpec((n_chunks, w, SSM_CHUNK), lambda i: (i, 0, 0))
    hm_shape = jax.ShapeDtypeStruct((t // SSM_CHUNK, w, SSM_CHUNK), F32)
    split_spec = pl.BlockSpec((3, tb, w), lambda i: (0, i, 0))
    split_shape = jax.ShapeDtypeStruct((3, t, w), BF16)
    blocks = 3 * _nbytes((tb, w), F32) + 3 * _nbytes((n_chunks, w, SSM_CHUNK), F32)
    return pl.pallas_call(
        functools.partial(_dt_kernel, n_chunks=n_chunks, n_heads=w // 2),
        grid=(t // tb,),
        in_specs=[
            tok_spec,
            pl.BlockSpec((1, w), lambda i: (0, 0)),
            pl.BlockSpec((1, w), lambda i: (0, 0)),
        ],
        out_specs=[tok_spec, split_spec, split_spec, hm_spec, hm_spec],
        out_shape=[jax.ShapeDtypeStruct((t, w), F32), split_shape, split_shape, hm_shape, hm_shape],
        compiler_params=pltpu.CompilerParams(
            dimension_semantics=("parallel",),
            vmem_limit_bytes=_vmem_limit(blocks, 0, 4 << 20)),
        name=name,
    )(raw, dt_bias.reshape(1, w), a_log.reshape(1, w))


HALO = SUBLANES
SSD_SCAN_TB = 1024
SSD_SCAN_UNROLL = 4


def _any_equal(v, values):
    return functools.reduce(jnp.logical_or, [v == s for s in values])


def _conv_kernel(prev_ref, main_ref, next_ref, w_ref, b_ref, o_ref, *, tb, starts, ends):
    start = pl.program_id(0) * tb
    at_start = _any_equal(start, starts)
    at_end = _any_equal(start + tb, ends)
    buf = jnp.concatenate([jnp.where(at_start, 0.0, prev_ref[...]), main_ref[...],
                           jnp.where(at_end, 0.0, next_ref[...])], axis=0)
    n = tb + 2 * HALO
    acc = b_ref[...] + main_ref[...] * w_ref[SSM_CONV // 2:SSM_CONV // 2 + 1, :]
    for k in range(SSM_CONV):
        if k != SSM_CONV // 2:
            rolled = pltpu.roll(buf, (SSM_CONV // 2 - k) % n, 0)
            acc = acc + rolled[HALO:HALO + tb] * w_ref[k:k + 1, :]
    o_ref[...] = (acc / (1.0 + jnp.exp(-acc))).astype(o_ref.dtype)


def ssd_conv(zx, conv_w, conv_b, *, col0, tb, tc, starts, ends, name):
    t = zx.shape[0]
    c = conv_w.shape[1]
    assert t % tb == 0 and c % tc == 0 and col0 % tc == 0 and tb % HALO == 0
    cb0 = col0 // tc
    hb = tb // HALO
    n_hb = t // HALO
    blocks = (_nbytes((tb, tc), F32) + _nbytes((tb, tc), BF16))
    return pl.pallas_call(
        functools.partial(_conv_kernel, tb=tb, starts=starts, ends=ends),
        grid=(t // tb, c // tc),
        in_specs=[
            pl.BlockSpec((HALO, tc), lambda i, j: (jnp.maximum(i * hb - 1, 0), cb0 + j)),
            pl.BlockSpec((tb, tc), lambda i, j: (i, cb0 + j)),
            pl.BlockSpec((HALO, tc), lambda i, j: (jnp.minimum((i + 1) * hb, n_hb - 1), cb0 + j)),
            pl.BlockSpec((SSM_CONV, tc), lambda i, j: (0, j)),
            pl.BlockSpec((1, tc), lambda i, j: (0, j)),
        ],
        out_specs=pl.BlockSpec((tb, tc), lambda i, j: (i, j)),
        out_shape=jax.ShapeDtypeStruct((t, c), BF16),
        compiler_params=pltpu.CompilerParams(
            dimension_semantics=("parallel", "arbitrary"),
            vmem_limit_bytes=_vmem_limit(blocks, 0, 8 * _nbytes((tb, tc), F32))),
        name=name,
    )(zx, zx, zx, conv_w, conv_b.reshape(1, c))


def _scan_kernel(x_ref, b_ref, c_ref, a_ref, ea3_ref, w3_ref, atd_ref, eend_ref, *rest, tb, bounds, reverse):
    if reverse:
        yf_ref, z_ref, dsk_ref, nw_ref, o_ref, state_ref, y_ref = rest
    else:
        o_ref, state_ref = rest
        y_ref = o_ref
    L = SSM_CHUNK
    P = SSM_HEAD_DIM
    N = SSM_STATE
    n_chunks = tb // L
    heads = x_ref.shape[1] // P
    k3 = ea3_ref.shape[1]
    col_head = lax.broadcasted_iota(jnp.int32, (k3, heads * P), 0) % heads
    lane_head = lax.broadcasted_iota(jnp.int32, (k3, heads * P), 1) // P
    exp_head = jnp.where(col_head == lane_head, 1.0, 0.0).astype(BF16)
    i = pl.program_id(1)
    blk = (pl.num_programs(1) - 1 - i) if reverse else i
    edge = (blk + 1) * tb if reverse else blk * tb

    @pl.when(_any_equal(edge, bounds))
    def _():
        state_ref[...] = jnp.zeros_like(state_ref)

    row = lax.broadcasted_iota(jnp.int32, (L, L), 0)
    col = lax.broadcasted_iota(jnp.int32, (L, L), 1)
    tri = (row <= col) if reverse else (row >= col)
    first_head = lax.broadcasted_iota(jnp.int32, (L, LANES), 1) < P
    first_head_n = lax.broadcasted_iota(jnp.int32, (N, LANES), 1) < P

    def chunk(j, carry):
        c = (n_chunks - 1 - j) if reverse else j
        rows = pl.ds(pl.multiple_of(c * L, L), L)
        x = x_ref[rows, :]
        bm = b_ref[rows, :]
        cm = c_ref[rows, :]
        atd = atd_ref[c]
        eend = eend_ref[c]
        a = a_ref[rows, :]
        ea_x = jnp.dot(ea3_ref[rows, :], exp_head, preferred_element_type=F32)
        w_x = jnp.dot(w3_ref[rows, :], exp_head, preferred_element_type=F32)
        cb = lax.dot_general(cm, bm, (((1,), (1,)), ((), ())), preferred_element_type=F32)
        y_off = jnp.dot(cm, state_ref[...].astype(BF16), preferred_element_type=F32) * ea_x
        xw = (x.astype(F32) * w_x).astype(BF16)
        z_new = lax.dot_general(bm, xw, (((0,), (0,)), ((), ())), preferred_element_type=F32)
        for pair in range(heads // 2):
            cols = slice(pair * LANES, (pair + 1) * LANES)
            xp = x[:, cols]
            ys = []
            for h in (2 * pair, 2 * pair + 1):
                seg = a[:, h:h + 1] - atd[h:h + 1, :]
                m = cb * jnp.exp2(jnp.where(tri, seg, -jnp.inf))
                ys.append(jnp.dot(m.astype(BF16), xp, preferred_element_type=F32))
            e_pair = jnp.where(first_head_n[:1], eend[2 * pair:2 * pair + 1, :], eend[2 * pair + 1:2 * pair + 2, :])
            state_ref[:, cols] = state_ref[:, cols] * e_pair + z_new[:, cols]
            y_pair = jnp.where(first_head, ys[0], ys[1]) + y_off[:, cols]
            if reverse:
                y_ref[:, cols] = y_pair
            else:
                y_ref[rows, cols] = y_pair

        if reverse:
            y = y_ref[...] + yf_ref[rows, :] + x.astype(F32) * dsk_ref[...]
            zz = z_ref[rows, :]
            y = y * (zz / (1.0 + jnp.exp(-zz)))
            o_ref[rows, :] = (_rms_scale(y) * nw_ref[...]).astype(o_ref.dtype)
        return carry

    lax.fori_loop(0, n_chunks, chunk, 0, unroll=min(SSD_SCAN_UNROLL, n_chunks))


def ssd_scan(xc, ag, ea3g, w3g, atd, eend, extra, *, d_inner, tb, bounds, reverse, name):
    t = xc.shape[0]
    G = SSM_GROUPS
    gw = d_inner // G
    hpg = gw // SSM_HEAD_DIM
    n_blk = t // tb
    n_chunks = tb // SSM_CHUNK
    assert t % tb == 0 and gw % LANES == 0
    xb0 = d_inner // SSM_STATE
    cb0 = xb0 + G
    at_row0 = (atd.shape[1] // 2 // hpg) if reverse else 0

    def tok(i):
        return (n_blk - 1 - i) if reverse else i

    direction = 1 if reverse else 0
    tok_spec = pl.BlockSpec((None, None, tb, hpg), lambda g, i: (direction, g, tok(i), 0))
    tok3_spec = pl.BlockSpec((None, None, tb, 3 * hpg), lambda g, i: (direction, g, tok(i), 0))
    hm_spec = pl.BlockSpec((n_chunks, hpg, SSM_CHUNK), lambda g, i: (tok(i), at_row0 + g, 0))
    in_specs = [
        pl.BlockSpec((tb, gw), lambda g, i: (tok(i), g)),
        pl.BlockSpec((tb, SSM_STATE), lambda g, i: (tok(i), xb0 + g)),
        pl.BlockSpec((tb, SSM_STATE), lambda g, i: (tok(i), cb0 + g)),
        tok_spec, tok3_spec, tok3_spec, hm_spec, hm_spec,
    ]
    args = [xc, xc, xc, ag, ea3g, w3g, atd, eend]
    blocks = (_nbytes((tb, gw), BF16) + 2 * _nbytes((tb, SSM_STATE), BF16)
              + 3 * _nbytes((tb, LANES), BF16) + 2 * _nbytes((n_chunks, hpg, SSM_CHUNK), F32)
              + _nbytes((tb, gw), F32))
    if reverse:
        yf, zx, dsk, nw = extra
        in_specs += [
            pl.BlockSpec((tb, gw), lambda g, i: (tok(i), g)),
            pl.BlockSpec((tb, gw), lambda g, i: (tok(i), g)),
            pl.BlockSpec((1, gw), lambda g, i: (0, g)),
            pl.BlockSpec((1, gw), lambda g, i: (0, g)),
        ]
        args += [yf, zx, dsk, nw]
        blocks += 2 * _nbytes((tb, gw), F32)
    scratch = [pltpu.VMEM((SSM_STATE, gw), F32)]
    if reverse:
        scratch.append(pltpu.VMEM((SSM_CHUNK, gw), F32))
    return pl.pallas_call(
        functools.partial(_scan_kernel, tb=tb, bounds=bounds, reverse=reverse),
        grid=(G, n_blk),
        in_specs=in_specs,
        out_specs=pl.BlockSpec((tb, gw), lambda g, i: (tok(i), g)),
        out_shape=jax.ShapeDtypeStruct((t, d_inner), BF16 if reverse else F32),
        scratch_shapes=scratch,
        compiler_params=pltpu.CompilerParams(
            dimension_semantics=("parallel", "arbitrary"),
            vmem_limit_bytes=_vmem_limit(blocks, _nbytes((SSM_STATE, gw), F32), 8 << 20)),
        name=name,
    )(*args)


def _mixer_na(x, nw, qkv_w, qkv_b, rpb, out_w, out_b, seqs, li):
    d = x.shape[1]
    q_scale = jnp.concatenate([jnp.full((d,), NA_HEAD_DIM ** -0.5 * LOG2_E, F32), jnp.ones((2 * d,), F32)])
    qkv = norm_matmul(x, nw, qkv_w, BF16, b=qkv_b, scale=q_scale, tm=1024, tn=1024, name=f"na_qkv_{li}")
    rp = _na_pad_rpb(rpb)
    outs = []
    tok0 = 0
    for si, (seq_len, n_seq) in enumerate(seqs):
        assert tok0 % seq_len == 0
        outs.append(na_core(qkv, rp, seq_len=seq_len, seq_block0=tok0 // seq_len,
                            n_seq=n_seq, name=f"na_core_{li}_{si}"))
        tok0 += seq_len * n_seq
    return matmul_res(tuple(outs), out_w, x, b=out_b, tm=1024, tn=1024, name=f"na_out_{li}")


def _mixer_ssd(x, nw, in_w, conv_w, conv_b, dt_bias, a_log, d_skip, norm_w, out_w, starts, ends, li):
    d_inner = out_w.shape[0]
    conv_dim = conv_w.shape[1]
    n_main = d_inner + conv_dim
    n_dt = in_w.shape[1] - n_main
    heads = d_inner // SSM_HEAD_DIM
    hpg = heads // SSM_GROUPS
    t = x.shape[0]
    zx, raw = norm_matmul(x, nw, in_w, F32, n=n_main, side=(n_main, n_dt), tm=1024, tn=1024,
                          name=f"ssd_in_{li}")
    a, ea3, w3, atd, eend = ssd_steps(raw, dt_bias, a_log, tb=1024, name=f"ssd_steps_{li}")
    xc = ssd_conv(zx, conv_w, conv_b, col0=d_inner, tb=512, tc=512, starts=starts, ends=ends, name=f"ssd_conv_{li}")

    def per_group(v):
        parts = v.shape[0]
        g = jnp.transpose(v.reshape(parts, t, 2, SSM_GROUPS, hpg), (2, 3, 1, 0, 4))
        return g.reshape(2, SSM_GROUPS, t, parts * hpg)

    ag = per_group(a[None])
    ea3g = per_group(ea3)
    w3g = per_group(w3)
    yf = ssd_scan(xc, ag, ea3g, w3g, atd, eend, None, d_inner=d_inner, tb=SSD_SCAN_TB,
                  bounds=starts, reverse=False, name=f"ssd_fwd_{li}")
    dsk = jnp.repeat(d_skip.astype(F32), SSM_HEAD_DIM).reshape(1, d_inner)
    y = ssd_scan(xc, ag, ea3g, w3g, atd, eend, (yf, zx, dsk, norm_w.reshape(1, d_inner)),
                 d_inner=d_inner, tb=SSD_SCAN_TB, bounds=ends, reverse=True, name=f"ssd_bwd_{li}")
    return matmul_res(y, out_w, x, tm=1024, tn=1024, name=f"ssd_out_{li}")


def kernel(x_prompt, x_sample, mix_norm, na_qkv_w, na_qkv_b, na_rpb, na_out_w, na_out_b, ssm_in_w, ssm_conv_w, ssm_conv_b, ssm_dt_bias, ssm_a_log, ssm_d, ssm_norm_w, ssm_out_w, mlp_norm, mlp_up_w, mlp_down_w, final_norm):
    bp, tp, d = x_prompt.shape
    bs, ts, _ = x_sample.shape
    depth = mix_norm.shape[0]
    x = jnp.concatenate([x_prompt.reshape(bp * tp, d), x_sample.reshape(bs * ts, d)], axis=0)
    seqs = ((tp, bp), (ts, bs))
    starts, ends = [], []
    tok = 0
    for seq_len, n_seq in seqs:
        for _ in range(n_seq):
            starts.append(tok)
            tok += seq_len
            ends.append(tok)
    starts, ends = tuple(starts), tuple(ends)

    for i in range(depth):
        j = i // N_MIXERS
        if i % N_MIXERS == 0:
            x = _mixer_na(x, mix_norm[i], cast_bf16(na_qkv_w, j, name=f"cast_qkv_{i}"), na_qkv_b[j], na_rpb[j],
                          cast_bf16(na_out_w, j, name=f"cast_na_out_{i}"), na_out_b[j], seqs, i)
        else:
            x = _mixer_ssd(x, mix_norm[i], cast_bf16(ssm_in_w, j, name=f"cast_ssd_in_{i}"), ssm_conv_w[j],
                           ssm_conv_b[j], ssm_dt_bias[j], ssm_a_log[j], ssm_d[j], ssm_norm_w[j],
                           cast_bf16(ssm_out_w, j, name=f"cast_ssd_out_{i}"), starts, ends, i)
        x = mlp(x, mlp_norm[i], cast_bf16(mlp_up_w, i, name=f"cast_up_{i}"),
                cast_bf16(mlp_down_w, i, name=f"cast_down_{i}"), final_norm,
                final=(i == depth - 1), tm=1024, th=512, name=f"mlp_{i}")
    y_prompt = x[:bp * tp].reshape(bp, tp, d)
    y_sample = x[bp * tp:].reshape(bs, ts, d)
    return (y_prompt, y_sample)
```

```python
import functools

import numpy as np
import jax
import jax.numpy as jnp
from jax import lax
from jax.experimental import pallas as pl
from jax.experimental.pallas import tpu as pltpu

F32 = jnp.float32
BF16 = jnp.bfloat16

GRID_W = 64
NA_HEADS = 32
NA_HEAD_DIM = 64
NA_KH = 8
NA_KW = 16
SSM_HEAD_DIM = 64
SSM_GROUPS = 8
SSM_STATE = 128
SSM_CONV = 5
SSM_CHUNK = 128
N_MIXERS = 2
RMS_EPS = 1e-5

LANES = 128
SUBLANES = 8
VMEM_BYTES = 64 * 1024 * 1024
VMEM_CAP = VMEM_BYTES - 8 * 1024 * 1024


def _vmem_limit(pipelined_bytes, scratch_bytes=0, temp_bytes=0):
    need = 2 * pipelined_bytes + scratch_bytes + temp_bytes + (4 << 20)
    return int(min(max(need, 16 << 20), VMEM_CAP))


def _nbytes(shape, dtype):
    return int(np.prod(shape)) * jnp.dtype(dtype).itemsize


def _rms_scale(x):
    return x * lax.rsqrt(jnp.mean(x * x, axis=-1, keepdims=True) + RMS_EPS)


CAST_BLOCK_BYTES = 4 << 20


def _cast_kernel(w_ref, o_ref):
    o_ref[...] = w_ref[...].astype(o_ref.dtype)


def cast_bf16(w3, layer, *, name):
    _, k, n = w3.shape
    tk = k
    while tk % 2 == 0 and tk > 2 * SUBLANES and _nbytes((tk, n), F32) > CAST_BLOCK_BYTES:
        tk //= 2
    return pl.pallas_call(
        _cast_kernel,
        grid=(k // tk,),
        in_specs=[pl.BlockSpec((None, tk, n), lambda i: (layer, i, 0))],
        out_specs=pl.BlockSpec((tk, n), lambda i: (i, 0)),
        out_shape=jax.ShapeDtypeStruct((k, n), BF16),
        compiler_params=pltpu.CompilerParams(
            dimension_semantics=("parallel",),
            vmem_limit_bytes=_vmem_limit(_nbytes((tk, n), F32) + _nbytes((tk, n), BF16))),
        name=name,
    )(w3)


def _as_parts(v):
    return tuple(v) if isinstance(v, (tuple, list)) else (v,)


def _stacked_specs(parts, block, col_of):
    specs, starts = [], []
    off = 0
    for arr in parts:
        nb = arr.shape[0] // block[0]
        assert arr.shape[0] % block[0] == 0
        specs.append(pl.BlockSpec(block, lambda i, j, off=off, nb=nb: (jnp.clip(i - off, 0, nb - 1), col_of(j))))
        starts.append(off)
        off += nb
    return specs, tuple(starts), off


def _stacked_read(refs, starts):
    i = pl.program_id(0)
    v = refs[0][...]
    for r, s in zip(refs[1:], starts[1:]):
        v = jnp.where(i >= s, r[...], v)
    return v


def _norm_matmul_kernel(x_ref, nw_ref, w_ref, *refs, has_bias, has_scale, has_side):
    refs = list(refs)
    ws_ref = refs.pop(0) if has_side else None
    b_ref = refs.pop(0) if has_bias else None
    s_ref = refs.pop(0) if has_scale else None
    o_ref = refs.pop(0)
    os_ref = refs.pop(0) if has_side else None
    (hn_ref,) = refs

    @pl.when(pl.program_id(1) == 0)
    def _():
        hn = (_rms_scale(x_ref[...]) * nw_ref[...]).astype(BF16)
        hn_ref[...] = hn
        if has_side:
            os_ref[...] = jnp.dot(hn, ws_ref[...], preferred_element_type=F32)

    acc = jnp.dot(hn_ref[...], w_ref[...], preferred_element_type=F32)
    if has_bias:
        acc = acc + b_ref[...]
    if has_scale:
        acc = acc * s_ref[...]
    o_ref[...] = acc.astype(o_ref.dtype)


def norm_matmul(x, nw, w, out_dtype, *, b=None, scale=None, n=None, side=None, tm, tn, name):
    t, d = x.shape
    n = w.shape[1] if n is None else n
    assert t % tm == 0 and n % tn == 0
    blocks = (_nbytes((tm, d), F32) + _nbytes((d, tn), BF16) + _nbytes((tm, tn), out_dtype))
    vec_spec = pl.BlockSpec((1, tn), lambda i, j: (0, j))
    in_specs = [
        pl.BlockSpec((tm, d), lambda i, j: (i, 0)),
        pl.BlockSpec((1, d), lambda i, j: (0, 0)),
        pl.BlockSpec((d, tn), lambda i, j: (0, j)),
    ]
    args = [x, nw.reshape(1, d), w]
    out_specs = [pl.BlockSpec((tm, tn), lambda i, j: (i, j))]
    out_shape = [jax.ShapeDtypeStruct((t, n), out_dtype)]
    if side is not None:
        col0, ns = side
        assert col0 % ns == 0
        in_specs.append(pl.BlockSpec((d, ns), lambda i, j: (0, col0 // ns)))
        args.append(w)
        out_specs.append(pl.BlockSpec((tm, ns), lambda i, j: (i, 0)))
        out_shape.append(jax.ShapeDtypeStruct((t, ns), F32))
        blocks += _nbytes((d, ns), BF16) + _nbytes((tm, ns), F32)
    for v in (b, scale):
        if v is not None:
            in_specs.append(vec_spec)
            args.append(v.reshape(1, n))
    outs = pl.pallas_call(
        functools.partial(_norm_matmul_kernel, has_bias=b is not None, has_scale=scale is not None,
                          has_side=side is not None),
        grid=(t // tm, n // tn),
        in_specs=in_specs,
        out_specs=out_specs,
        out_shape=out_shape,
        scratch_shapes=[pltpu.VMEM((tm, d), BF16)],
        compiler_params=pltpu.CompilerParams(
            dimension_semantics=("parallel", "arbitrary"),
            vmem_limit_bytes=_vmem_limit(blocks, _nbytes((tm, d), BF16), _nbytes((tm, tn), F32))),
        name=name,
    )(*args)
    return outs if side is not None else outs[0]


def _matmul_res_kernel(*refs, a_starts, r_starts, has_bias):
    refs = list(refs)
    a_refs = [refs.pop(0) for _ in a_starts]
    w_ref = refs.pop(0)
    r_refs = [refs.pop(0) for _ in r_starts]
    b_ref = refs.pop(0) if has_bias else None
    (o_ref,) = refs
    acc = jnp.dot(_stacked_read(a_refs, a_starts), w_ref[...], preferred_element_type=F32)
    if has_bias:
        acc = acc + b_ref[...]
    o_ref[...] = _stacked_read(r_refs, r_starts) + acc


def matmul_res(a, w, res, *, b=None, tm, tn, name):
    a_parts, r_parts = _as_parts(a), _as_parts(res)
    k, n = w.shape
    assert n % tn == 0
    a_specs, a_starts, n_i = _stacked_specs(a_parts, (tm, k), lambda j: 0)
    r_specs, r_starts, n_r = _stacked_specs(r_parts, (tm, tn), lambda j: j)
    assert n_i == n_r
    blocks = (len(a_parts) * _nbytes((tm, k), BF16) + _nbytes((k, tn), BF16)
              + (len(r_parts) + 1) * _nbytes((tm, tn), F32))
    extra = [] if b is None else [b.reshape(1, n)]
    return pl.pallas_call(
        functools.partial(_matmul_res_kernel, a_starts=a_starts, r_starts=r_starts, has_bias=b is not None),
        grid=(n_i, n // tn),
        in_specs=a_specs + [pl.BlockSpec((k, tn), lambda i, j: (0, j))] + r_specs
        + [pl.BlockSpec((1, tn), lambda i, j: (0, j))] * len(extra),
        out_specs=pl.BlockSpec((tm, tn), lambda i, j: (i, j)),
        out_shape=jax.ShapeDtypeStruct((n_i * tm, n), F32),
        compiler_params=pltpu.CompilerParams(
            dimension_semantics=("parallel", "arbitrary"),
            vmem_limit_bytes=_vmem_limit(blocks, 0, _nbytes((tm, tn), F32))),
        name=name,
    )(*a_parts, w, *r_parts, *extra)


def _mlp_kernel(x_ref, nw_ref, wu_ref, wd_ref, fw_ref, o_ref, hn_ref, *, final):
    k = pl.program_id(1)

    @pl.when(k == 0)
    def _():
        x = x_ref[...]
        hn_ref[...] = (_rms_scale(x) * nw_ref[...]).astype(BF16)
        o_ref[...] = x

    h = jnp.dot(hn_ref[...], wu_ref[...], preferred_element_type=F32)
    h = jnp.square(jnp.maximum(h, 0.0)).astype(BF16)
    o_ref[...] += jnp.dot(h, wd_ref[...], preferred_element_type=F32)

    if final:
        @pl.when(k == pl.num_programs(1) - 1)
        def _():
            o_ref[...] = _rms_scale(o_ref[...]) * fw_ref[...]


def mlp(x, nw, wu, wd, fw, *, final, tm, th, name):
    t, d = x.shape
    hid = wu.shape[1]
    assert t % tm == 0 and hid % th == 0
    blocks = (2 * _nbytes((tm, d), F32) + _nbytes((d, th), BF16) + _nbytes((th, d), BF16))
    return pl.pallas_call(
        functools.partial(_mlp_kernel, final=final),
        grid=(t // tm, hid // th),
        in_specs=[
            pl.BlockSpec((tm, d), lambda i, k: (i, 0)),
            pl.BlockSpec((1, d), lambda i, k: (0, 0)),
            pl.BlockSpec((d, th), lambda i, k: (0, k)),
            pl.BlockSpec((th, d), lambda i, k: (k, 0)),
            pl.BlockSpec((1, d), lambda i, k: (0, 0)),
        ],
        out_specs=pl.BlockSpec((tm, d), lambda i, k: (i, 0)),
        out_shape=jax.ShapeDtypeStruct((t, d), F32),
        scratch_shapes=[pltpu.VMEM((tm, d), BF16)],
        compiler_params=pltpu.CompilerParams(
            dimension_semantics=("parallel", "arbitrary"),
            vmem_limit_bytes=_vmem_limit(blocks, _nbytes((tm, d), BF16),
                                         _nbytes((tm, th), F32) + _nbytes((tm, d), F32))),
        name=name,
    )(x, nw.reshape(1, d), wu, wd, fw.reshape(1, d))


LOG2_E = 1.4426950408889634
NA_WIN_TOK = NA_KH * GRID_W
NA_N_DR = 2 * NA_KH - 1
NA_ROWS_PER_ITER = 8


def _na_kernel(q_ref, k_ref, v_ref, rp_ref, o_ref, u_ref, s_ref, p_ref, l_ref, *, rows):
    W = GRID_W
    lane = lax.broadcasted_iota(jnp.int32, (W, LANES), 1)
    first_head = lane < NA_HEAD_DIM
    even_row = lane < W

    c = lax.broadcasted_iota(jnp.int32, (W, LANES), 0)
    kc = jnp.where(even_row, lane, lane - W)
    cs = jnp.clip(c - NA_KW // 2, 0, W - NA_KW)
    valid = (kc >= cs) & (kc < cs + NA_KW)
    shift = LANES - (NA_KW - 1)
    for h in range(2):
        t0_prev = None
        for dr in range(NA_N_DR):
            base = jnp.broadcast_to(rp_ref[h, dr:dr + 1, :] * LOG2_E, (W, LANES))
            t0 = pltpu.roll(base, shift, 1, stride=1, stride_axis=0)
            if dr >= 1:
                t1 = pltpu.roll(base, (shift + W) % LANES, 1, stride=1, stride_axis=0)
                u_ref[h, dr - 1] = jnp.where(valid, jnp.where(even_row, t0_prev, t1), -jnp.inf)
            t0_prev = t0

    RB = NA_ROWS_PER_ITER
    n_groups = rows // RB

    def win_start(r):
        return jnp.clip(r - NA_KH // 2, 0, rows - NA_KH)

    def scores(g):
        for u in range(RB):
            r = g * RB + u
            q0 = pl.multiple_of(r * W, W)
            k0 = pl.multiple_of(win_start(r) * W, W)
            q = q_ref[pl.ds(q0, W), :]
            zero = jnp.zeros_like(q)
            qs = jnp.concatenate([jnp.where(first_head, q, zero), jnp.where(first_head, zero, q)], axis=0)
            kk = k_ref[pl.ds(k0, NA_WIN_TOK), :]
            s_ref[u] = lax.dot_general(qs, kk, (((1,), (1,)), ((), ())), preferred_element_type=F32)

    def softmax(g):
        for u in range(RB):
            r = g * RB + u
            d0 = (NA_KH - 1) - (r - win_start(r))
            bias = jnp.concatenate(
                [jnp.concatenate([u_ref[h, d0 + 2 * j] for j in range(NA_KH // 2)], axis=1) for h in range(2)],
                axis=0)
            s = s_ref[u] + bias
            m = jnp.max(s, axis=-1, keepdims=True)
            p = jnp.exp2(s - m)
            p_ref[u] = p.astype(BF16)
            l_ref[u] = jnp.broadcast_to(jnp.sum(p, axis=-1, keepdims=True), (2 * W, LANES))

    def values(g):
        for u in range(RB):
            r = g * RB + u
            q0 = pl.multiple_of(r * W, W)
            k0 = pl.multiple_of(win_start(r) * W, W)
            vv = v_ref[pl.ds(k0, NA_WIN_TOK), :]
            pv = jnp.dot(p_ref[u], vv, preferred_element_type=F32) / l_ref[u]
            out = jnp.where(first_head, pv[:W], pv[W:])
            o_ref[pl.ds(q0, W), :] = out.astype(o_ref.dtype)

    scores(0)
    softmax(0)
    scores(1)

    def body(g, carry):
        values(g)
        softmax(g + 1)
        scores(g + 2)
        return carry

    lax.fori_loop(0, n_groups - 2, body, 0)
    values(n_groups - 2)
    softmax(n_groups - 1)
    values(n_groups - 1)


def na_core(qkv, rp, *, seq_len, seq_block0, n_seq, name):
    d = qkv.shape[1] // 3
    n_pairs = d // LANES
    rows = seq_len // GRID_W
    assert rows % NA_ROWS_PER_ITER == 0 and rows // NA_ROWS_PER_ITER >= 2 and rows >= NA_KH
    blk = (seq_len, LANES)
    u_shape = (2, NA_N_DR - 1, GRID_W, LANES)
    s_shape = (NA_ROWS_PER_ITER, 2 * GRID_W, NA_WIN_TOK)
    l_shape = (NA_ROWS_PER_ITER, 2 * GRID_W, LANES)
    scratch_bytes = (_nbytes(u_shape, F32) + _nbytes(s_shape, F32) + _nbytes(s_shape, BF16)
                     + _nbytes(l_shape, F32))
    blocks = 4 * _nbytes(blk, BF16) + _nbytes((2,) + rp.shape[1:], F32)
    return pl.pallas_call(
        functools.partial(_na_kernel, rows=rows),
        grid=(n_pairs, n_seq),
        in_specs=[
            pl.BlockSpec(blk, lambda h, b: (seq_block0 + b, h)),
            pl.BlockSpec(blk, lambda h, b: (seq_block0 + b, n_pairs + h)),
            pl.BlockSpec(blk, lambda h, b: (seq_block0 + b, 2 * n_pairs + h)),
            pl.BlockSpec((2,) + rp.shape[1:], lambda h, b: (h, 0, 0)),
        ],
        out_specs=pl.BlockSpec(blk, lambda h, b: (b, h)),
        out_shape=jax.ShapeDtypeStruct((n_seq * seq_len, d), BF16),
        scratch_shapes=[pltpu.VMEM(u_shape, F32), pltpu.VMEM(s_shape, F32), pltpu.VMEM(s_shape, BF16),
                        pltpu.VMEM(l_shape, F32)],
        compiler_params=pltpu.CompilerParams(
            dimension_semantics=("parallel", "arbitrary"),
            vmem_limit_bytes=_vmem_limit(blocks, scratch_bytes, 8 << 20)),
        name=name,
    )(qkv, qkv, qkv, rp)


def _na_pad_rpb(rpb):
    h, n_dr, n_dc = rpb.shape
    return jnp.pad(rpb.astype(F32), ((0, 0), (0, 2 * NA_KH - n_dr), (0, LANES - n_dc)))


def _split3(x):
    hi = x.astype(BF16)
    r1 = x - hi.astype(F32)
    mid = r1.astype(BF16)
    lo = (r1 - mid.astype(F32)).astype(BF16)
    return hi, mid, lo


def _dt_kernel(raw_ref, bias_ref, alog_ref, a_ref, ea_ref, w_ref, atd_ref, eend_ref, *, n_chunks, n_heads):
    L = SSM_CHUNK
    x = raw_ref[...] + bias_ref[...]
    dt = jnp.maximum(x, 0.0) + jnp.log1p(jnp.exp(-jnp.abs(x)))
    log2_dt = jnp.log(dt) * LOG2_E
    ad = dt * (-jnp.exp(alog_ref[...]))
    row = lax.broadcasted_iota(jnp.int32, (L, L), 0)
    col = lax.broadcasted_iota(jnp.int32, (L, L), 1)
    tri_f = (col <= row).astype(BF16)
    tri_b = (col >= row).astype(BF16)
    fwd_lane = lax.broadcasted_iota(jnp.int32, (L, 2 * n_heads), 1) < n_heads
    for c in range(n_chunks):
        rows = slice(c * L, (c + 1) * L)
        parts = _split3(ad[rows])
        f = sum(jnp.dot(tri_f, p, preferred_element_type=F32) for p in parts)
        b = sum(jnp.dot(tri_b, p, preferred_element_type=F32) for p in parts)
        a = jnp.where(fwd_lane, f, b)
        a_end = jnp.where(fwd_lane[:1], f[L - 1:L], b[0:1])
        a2 = a * LOG2_E
        w = dt[rows] * jnp.exp(a_end - a)
        a_ref[rows, :] = a2
        for part, (ve, vw) in enumerate(zip(_split3(jnp.exp(a)), _split3(w))):
            ea_ref[part, rows, :] = ve
            w_ref[part, rows, :] = vw
        atd_ref[c] = (a2 - log2_dt[rows]).T
        eend_ref[c] = jnp.broadcast_to(jnp.exp(a_end), a.shape).T


def ssd_steps(raw, dt_bias, a_log, *, tb, name):
    t, w = raw.shape
    n_chunks = tb // SSM_CHUNK
    assert t % tb == 0 and w == LANES
    tok_spec = pl.BlockSpec((tb, w), lambda i: (i, 0))
    hm_spec = pl.BlockSpec((n_chunks, w, SSM_CHUNK), lambda i: (i, 0, 0))
    hm_shape = jax.ShapeDtypeStruct((t // SSM_CHUNK, w, SSM_CHUNK), F32)
    split_spec = pl.BlockSpec((3, tb, w), lambda i: (0, i, 0))
    split_shape = jax.ShapeDtypeStruct((3, t, w), BF16)
    blocks = 3 * _nbytes((tb, w), F32) + 3 * _nbytes((n_chunks, w, SSM_CHUNK), F32)
    return pl.pallas_call(
        functools.partial(_dt_kernel, n_chunks=n_chunks, n_heads=w // 2),
        grid=(t // tb,),
        in_specs=[
            tok_spec,
            pl.BlockSpec((1, w), lambda i: (0, 0)),
            pl.BlockSpec((1, w), lambda i: (0, 0)),
        ],
        out_specs=[tok_spec, split_spec, split_spec, hm_spec, hm_spec],
        out_shape=[jax.ShapeDtypeStruct((t, w), F32), split_shape, split_shape, hm_shape, hm_shape],
        compiler_params=pltpu.CompilerParams(
            dimension_semantics=("parallel",),
            vmem_limit_bytes=_vmem_limit(blocks, 0, 4 << 20)),
        name=name,
    )(raw, dt_bias.reshape(1, w), a_log.reshape(1, w))


HALO = SUBLANES
SSD_SCAN_TB = 2048
SSD_SCAN_UNROLL = 4


def _any_equal(v, values):
    return functools.reduce(jnp.logical_or, [v == s for s in values])


def _conv_kernel(prev_ref, main_ref, next_ref, w_ref, b_ref, o_ref, *, tb, starts, ends):
    start = pl.program_id(0) * tb
    at_start = _any_equal(start, starts)
    at_end = _any_equal(start + tb, ends)
    buf = jnp.concatenate([jnp.where(at_start, 0.0, prev_ref[...]), main_ref[...],
                           jnp.where(at_end, 0.0, next_ref[...])], axis=0)
    n = tb + 2 * HALO
    acc = b_ref[...] + main_ref[...] * w_ref[SSM_CONV // 2:SSM_CONV // 2 + 1, :]
    for k in range(SSM_CONV):
        if k != SSM_CONV // 2:
            rolled = pltpu.roll(buf, (SSM_CONV // 2 - k) % n, 0)
            acc = acc + rolled[HALO:HALO + tb] * w_ref[k:k + 1, :]
    o_ref[...] = (acc / (1.0 + jnp.exp(-acc))).astype(o_ref.dtype)


def ssd_conv(zx, conv_w, conv_b, *, col0, tb, tc, starts, ends, name):
    t = zx.shape[0]
    c = conv_w.shape[1]
    assert t % tb == 0 and c % tc == 0 and col0 % tc == 0 and tb % HALO == 0
    cb0 = col0 // tc
    hb = tb // HALO
    n_hb = t // HALO
    blocks = (_nbytes((tb, tc), F32) + _nbytes((tb, tc), BF16))
    return pl.pallas_call(
        functools.partial(_conv_kernel, tb=tb, starts=starts, ends=ends),
        grid=(t // tb, c // tc),
        in_specs=[
            pl.BlockSpec((HALO, tc), lambda i, j: (jnp.maximum(i * hb - 1, 0), cb0 + j)),
            pl.BlockSpec((tb, tc), lambda i, j: (i, cb0 + j)),
            pl.BlockSpec((HALO, tc), lambda i, j: (jnp.minimum((i + 1) * hb, n_hb - 1), cb0 + j)),
            pl.BlockSpec((SSM_CONV, tc), lambda i, j: (0, j)),
            pl.BlockSpec((1, tc), lambda i, j: (0, j)),
        ],
        out_specs=pl.BlockSpec((tb, tc), lambda i, j: (i, j)),
        out_shape=jax.ShapeDtypeStruct((t, c), BF16),
        compiler_params=pltpu.CompilerParams(
            dimension_semantics=("parallel", "arbitrary"),
            vmem_limit_bytes=_vmem_limit(blocks, 0, 8 * _nbytes((tb, tc), F32))),
        name=name,
    )(zx, zx, zx, conv_w, conv_b.reshape(1, c))


def _scan_kernel(x_ref, b_ref, c_ref, a_ref, ea3_ref, w3_ref, atd_ref, eend_ref, *rest, tb, bounds, reverse):
    if reverse:
        yf_ref, z_ref, dsk_ref, nw_ref, o_ref, state_ref, y_ref = rest
    else:
        o_ref, state_ref = rest
        y_ref = o_ref
    L = SSM_CHUNK
    P = SSM_HEAD_DIM
    N = SSM_STATE
    n_chunks = tb // L
    heads = x_ref.shape[1] // P
    k3 = ea3_ref.shape[1]
    col_head = lax.broadcasted_iota(jnp.int32, (k3, heads * P), 0) % heads
    lane_head = lax.broadcasted_iota(jnp.int32, (k3, heads * P), 1) // P
    exp_head = jnp.where(col_head == lane_head, 1.0, 0.0).astype(BF16)
    i = pl.program_id(1)
    blk = (pl.num_programs(1) - 1 - i) if reverse else i
    edge = (blk + 1) * tb if reverse else blk * tb

    @pl.when(_any_equal(edge, bounds))
    def _():
        state_ref[...] = jnp.zeros_like(state_ref)

    row = lax.broadcasted_iota(jnp.int32, (L, L), 0)
    col = lax.broadcasted_iota(jnp.int32, (L, L), 1)
    tri = (row <= col) if reverse else (row >= col)
    first_head = lax.broadcasted_iota(jnp.int32, (L, LANES), 1) < P
    first_head_n = lax.broadcasted_iota(jnp.int32, (N, LANES), 1) < P

    def chunk(j, carry):
        c = (n_chunks - 1 - j) if reverse else j
        rows = pl.ds(pl.multiple_of(c * L, L), L)
        x = x_ref[rows, :]
        bm = b_ref[rows, :]
        cm = c_ref[rows, :]
        atd = atd_ref[c]
        eend = eend_ref[c]
        a = a_ref[rows, :]
        ea_x = jnp.dot(ea3_ref[rows, :], exp_head, preferred_element_type=F32)
        w_x = jnp.dot(w3_ref[rows, :], exp_head, preferred_element_type=F32)
        cb = lax.dot_general(cm, bm, (((1,), (1,)), ((), ())), preferred_element_type=F32)
        y_off = jnp.dot(cm, state_ref[...].astype(BF16), preferred_element_type=F32) * ea_x
        xw = (x.astype(F32) * w_x).astype(BF16)
        z_new = lax.dot_general(bm, xw, (((0,), (0,)), ((), ())), preferred_element_type=F32)
        for pair in range(heads // 2):
            cols = slice(pair * LANES, (pair + 1) * LANES)
            xp = x[:, cols]
            ys = []
            for h in (2 * pair, 2 * pair + 1):
                seg = a[:, h:h + 1] - atd[h:h + 1, :]
                m = cb * jnp.exp2(jnp.where(tri, seg, -jnp.inf))
                ys.append(jnp.dot(m.astype(BF16), xp, preferred_element_type=F32))
            e_pair = jnp.where(first_head_n[:1], eend[2 * pair:2 * pair + 1, :], eend[2 * pair + 1:2 * pair + 2, :])
            state_ref[:, cols] = state_ref[:, cols] * e_pair + z_new[:, cols]
            y_pair = jnp.where(first_head, ys[0], ys[1]) + y_off[:, cols]
            if reverse:
                y_ref[:, cols] = y_pair
            else:
                y_ref[rows, cols] = y_pair

        if reverse:
            y = y_ref[...] + yf_ref[rows, :] + x.astype(F32) * dsk_ref[...]
            zz = z_ref[rows, :]
            y = y * (zz / (1.0 + jnp.exp(-zz)))
            o_ref[rows, :] = (_rms_scale(y) * nw_ref[...]).astype(o_ref.dtype)
        return carry

    lax.fori_loop(0, n_chunks, chunk, 0, unroll=min(SSD_SCAN_UNROLL, n_chunks))


def ssd_scan(xc, ag, ea3g, w3g, atd, eend, extra, *, d_inner, tb, bounds, reverse, name):
    t = xc.shape[0]
    G = SSM_GROUPS
    gw = d_inner // G
    hpg = gw // SSM_HEAD_DIM
    n_blk = t // tb
    n_chunks = tb // SSM_CHUNK
    assert t % tb == 0 and gw % LANES == 0
    xb0 = d_inner // SSM_STATE
    cb0 = xb0 + G
    at_row0 = (atd.shape[1] // 2 // hpg) if reverse else 0

    def tok(i):
        return (n_blk - 1 - i) if reverse else i

    direction = 1 if reverse else 0
    tok_spec = pl.BlockSpec((None, None, tb, hpg), lambda g, i: (direction, g, tok(i), 0))
    tok3_spec = pl.BlockSpec((None, None, tb, 3 * hpg), lambda g, i: (direction, g, tok(i), 0))
    hm_spec = pl.BlockSpec((n_chunks, hpg, SSM_CHUNK), lambda g, i: (tok(i), at_row0 + g, 0))
    in_specs = [
        pl.BlockSpec((tb, gw), lambda g, i: (tok(i), g)),
        pl.BlockSpec((tb, SSM_STATE), lambda g, i: (tok(i), xb0 + g)),
        pl.BlockSpec((tb, SSM_STATE), lambda g, i: (tok(i), cb0 + g)),
        tok_spec, tok3_spec, tok3_spec, hm_spec, hm_spec,
    ]
    args = [xc, xc, xc, ag, ea3g, w3g, atd, eend]
    blocks = (_nbytes((tb, gw), BF16) + 2 * _nbytes((tb, SSM_STATE), BF16)
              + 3 * _nbytes((tb, LANES), BF16) + 2 * _nbytes((n_chunks, hpg, SSM_CHUNK), F32)
              + _nbytes((tb, gw), F32))
    if reverse:
        yf, zx, dsk, nw = extra
        in_specs += [
            pl.BlockSpec((tb, gw), lambda g, i: (tok(i), g)),
            pl.BlockSpec((tb, gw), lambda g, i: (tok(i), g)),
            pl.BlockSpec((1, gw), lambda g, i: (0, g)),
            pl.BlockSpec((1, gw), lambda g, i: (0, g)),
        ]
        args += [yf, zx, dsk, nw]
        blocks += 2 * _nbytes((tb, gw), F32)
    scratch = [pltpu.VMEM((SSM_STATE, gw), F32)]
    if reverse:
        scratch.append(pltpu.VMEM((SSM_CHUNK, gw), F32))
    return pl.pallas_call(
        functools.partial(_scan_kernel, tb=tb, bounds=bounds, reverse=reverse),
        grid=(G, n_blk),
        in_specs=in_specs,
        out_specs=pl.BlockSpec((tb, gw), lambda g, i: (tok(i), g)),
        out_shape=jax.ShapeDtypeStruct((t, d_inner), BF16 if reverse else F32),
        scratch_shapes=scratch,
        compiler_params=pltpu.CompilerParams(
            dimension_semantics=("parallel", "arbitrary"),
            vmem_limit_bytes=_vmem_limit(blocks, _nbytes((SSM_STATE, gw), F32), 8 << 20)),
        name=name,
    )(*args)


def _mixer_na(x, nw, qkv_w, qkv_b, rpb, out_w, out_b, seqs, li):
    d = x.shape[1]
    q_scale = jnp.concatenate([jnp.full((d,), NA_HEAD_DIM ** -0.5 * LOG2_E, F32), jnp.ones((2 * d,), F32)])
    qkv = norm_matmul(x, nw, qkv_w, BF16, b=qkv_b, scale=q_scale, tm=1024, tn=1024, name=f"na_qkv_{li}")
    rp = _na_pad_rpb(rpb)
    outs = []
    tok0 = 0
    for si, (seq_len, n_seq) in enumerate(seqs):
        assert tok0 % seq_len == 0
        outs.append(na_core(qkv, rp, seq_len=seq_len, seq_block0=tok0 // seq_len,
                            n_seq=n_seq, name=f"na_core_{li}_{si}"))
        tok0 += seq_len * n_seq
    return matmul_res(tuple(outs), out_w, x, b=out_b, tm=1024, tn=1024, name=f"na_out_{li}")


def _mixer_ssd(x, nw, in_w, conv_w, conv_b, dt_bias, a_log, d_skip, norm_w, out_w, starts, ends, li):
    d_inner = out_w.shape[0]
    conv_dim = conv_w.shape[1]
    n_main = d_inner + conv_dim
    n_dt = in_w.shape[1] - n_main
    heads = d_inner // SSM_HEAD_DIM
    hpg = heads // SSM_GROUPS
    t = x.shape[0]
    zx, raw = norm_matmul(x, nw, in_w, F32, n=n_main, side=(n_main, n_dt), tm=1024, tn=1024,
                          name=f"ssd_in_{li}")
    a, ea3, w3, atd, eend = ssd_steps(raw, dt_bias, a_log, tb=1024, name=f"ssd_steps_{li}")
    xc = ssd_conv(zx, conv_w, conv_b, col0=d_inner, tb=1024, tc=1024, starts=starts, ends=ends,
                  name=f"ssd_conv_{li}")

    def per_group(v):
        parts = v.shape[0]
        g = jnp.transpose(v.reshape(parts, t, 2, SSM_GROUPS, hpg), (2, 3, 1, 0, 4))
        return g.reshape(2, SSM_GROUPS, t, parts * hpg)

    ag = per_group(a[None])
    ea3g = per_group(ea3)
    w3g = per_group(w3)
    yf = ssd_scan(xc, ag, ea3g, w3g, atd, eend, None, d_inner=d_inner, tb=SSD_SCAN_TB,
                  bounds=starts, reverse=False, name=f"ssd_fwd_{li}")
    dsk = jnp.repeat(d_skip.astype(F32), SSM_HEAD_DIM).reshape(1, d_inner)
    y = ssd_scan(xc, ag, ea3g, w3g, atd, eend, (yf, zx, dsk, norm_w.reshape(1, d_inner)),
                 d_inner=d_inner, tb=SSD_SCAN_TB, bounds=ends, reverse=True, name=f"ssd_bwd_{li}")
    return matmul_res(y, out_w, x, tm=1024, tn=1024, name=f"ssd_out_{li}")


def kernel(x_prompt, x_sample, mix_norm, na_qkv_w, na_qkv_b, na_rpb, na_out_w, na_out_b, ssm_in_w, ssm_conv_w, ssm_conv_b, ssm_dt_bias, ssm_a_log, ssm_d, ssm_norm_w, ssm_out_w, mlp_norm, mlp_up_w, mlp_down_w, final_norm):
    bp, tp, d = x_prompt.shape
    bs, ts, _ = x_sample.shape
    depth = mix_norm.shape[0]
    x = jnp.concatenate([x_prompt.reshape(bp * tp, d), x_sample.reshape(bs * ts, d)], axis=0)
    seqs = ((tp, bp), (ts, bs))
    starts, ends = [], []
    tok = 0
    for seq_len, n_seq in seqs:
        for _ in range(n_seq):
            starts.append(tok)
            tok += seq_len
            ends.append(tok)
    starts, ends = tuple(starts), tuple(ends)

    for i in range(depth):
        j = i // N_MIXERS
        if i % N_MIXERS == 0:
            x = _mixer_na(x, mix_norm[i], cast_bf16(na_qkv_w, j, name=f"cast_qkv_{i}"), na_qkv_b[j], na_rpb[j],
                          cast_bf16(na_out_w, j, name=f"cast_na_out_{i}"), na_out_b[j], seqs, i)
        else:
            x = _mixer_ssd(x, mix_norm[i], cast_bf16(ssm_in_w, j, name=f"cast_ssd_in_{i}"), ssm_conv_w[j],
                           ssm_conv_b[j], ssm_dt_bias[j], ssm_a_log[j], ssm_d[j], ssm_norm_w[j],
                           cast_bf16(ssm_out_w, j, name=f"cast_ssd_out_{i}"), starts, ends, i)
        x = mlp(x, mlp_norm[i], cast_bf16(mlp_up_w, i, name=f"cast_up_{i}"),
                cast_bf16(mlp_down_w, i, name=f"cast_down_{i}"), final_norm,
                final=(i == depth - 1), tm=1024, th=512, name=f"mlp_{i}")
    y_prompt = x[:bp * tp].reshape(bp, tp, d)
    y_sample = x[bp * tp:].reshape(bs, ts, d)
    return (y_prompt, y_sample)
```
